```python
import math
import jax
import jax.numpy as jnp
from jax import lax
import numpy as np

D_MODEL = 1024
BATCH = 8
SEQ = 2048
DEPTH = 4
DEC_BATCH = 128
DEC_SEQ = 4
PAST_LEN = 8192
PAGE_SIZE = 128

N_BRANCH = 4
BRANCH_W = 256
H_A = 4
Q_LORA = 256
KV_LORA = 128
NOPE_A = 64
ROPE_A = 32
V_A = 64
ROPE_THETA = 10000.0
LRU_W = 256
LRU_BLOCKS = 4
LRU_BW = LRU_W // LRU_BLOCKS
CONV_W = 4
LRU_C = 8.0
H_C = 4
DK_C = 64
DV_C = 64
RET_CHUNK = 128
H_D = 4
DH_D = 32
DV_D = 64
T5_BUCKETS = 32
T5_MAX_DIST = 128
FF_DIM = 2816
PLE_DIM = 256
N_NORMS = 8
Q_BLOCK = 128
EPS = 1e-6
NEG_INF = -1e30
F32 = jnp.float32

IN_SIZES = (Q_LORA, KV_LORA, ROPE_A,
            LRU_W, LRU_W,
            H_C * DK_C, H_C * DK_C, H_C * DV_C, H_C * DV_C,
            H_D * 2 * DH_D, H_D * 2 * DH_D, H_D * DV_D,
            N_BRANCH * D_MODEL)
IN_COLS = sum(IN_SIZES)

kernel_name = 'hybrid_mla_rglru_retention_diffattn_step'


def _split(z, sizes):
    idx = np.cumsum(sizes)[:-1].tolist()
    return jnp.split(z, idx, axis=-1)


def _rmsnorm(x, g):
    xf = x.astype(F32)
    y = xf * lax.rsqrt(jnp.mean(xf * xf, axis=-1, keepdims=True) + EPS)
    return (y * g.astype(F32)).astype(x.dtype)


def _group_norm(x, g):
    xf = x.astype(F32)
    mu = jnp.mean(xf, axis=-1, keepdims=True)
    var = jnp.mean(jnp.square(xf - mu), axis=-1, keepdims=True)
    y = ((xf - mu) * lax.rsqrt(var + EPS)).reshape(x.shape[:2] + (-1,))
    return (y * g.astype(F32)).astype(x.dtype)


def _swiglu(x, w_gu, w_dn):
    g, u = jnp.split(x @ w_gu, 2, axis=-1)
    return (jax.nn.silu(g) * u) @ w_dn


def _rope(x, pos):
    d = x.shape[-1]
    half = d // 2
    freqs = ROPE_THETA ** (-jnp.arange(half, dtype=F32) / half)
    ang = pos.astype(F32)[:, None] * freqs[None, :]
    ang = ang.reshape((1, pos.shape[0]) + (1,) * (x.ndim - 3) + (half,))
    cos, sin = jnp.cos(ang), jnp.sin(ang)
    x1 = x[..., :half].astype(F32)
    x2 = x[..., half:].astype(F32)
    return jnp.concatenate([x1 * cos - x2 * sin, x1 * sin + x2 * cos], axis=-1).astype(x.dtype)


def _t5_bias(table, q_pos, k_pos):
    n = jnp.maximum(q_pos[:, None] - k_pos[None, :], 0)
    max_exact = T5_BUCKETS // 2
    nf = jnp.maximum(n, 1).astype(F32)
    large = max_exact + (jnp.log(nf / max_exact) / math.log(T5_MAX_DIST / max_exact)
                         * (T5_BUCKETS - max_exact)).astype(jnp.int32)
    large = jnp.minimum(large, T5_BUCKETS - 1)
    bucket = jnp.where(n < max_exact, n, large)
    return jnp.moveaxis(table[bucket], -1, 0).astype(F32)


def _map_qblocks(fn, qs, q_pos):
    t = q_pos.shape[0]
    qb = min(Q_BLOCK, t)
    nb = t // qb

    def to_blocks(a):
        return jnp.moveaxis(a.reshape((a.shape[0], nb, qb) + a.shape[2:]), 1, 0)

    out = lax.map(fn, tuple(to_blocks(a) for a in qs) + (q_pos.reshape(nb, qb),))
    out = jnp.moveaxis(out, 0, 1)
    return out.reshape((out.shape[0], t) + out.shape[3:])


def _mla_attention(q_lat, q_pe, ckv, kpe, q_pos, k_pos):
    scale = (NOPE_A + ROPE_A) ** -0.5

    def block(args):
        ql, qp, qpos = args
        s = (jnp.einsum('bqhl,bkl->bhqk', ql, ckv, preferred_element_type=F32)
             + jnp.einsum('bqhr,bkr->bhqk', qp, kpe, preferred_element_type=F32)) * scale
        s = jnp.where((k_pos[None, :] <= qpos[:, None])[None, None], s, NEG_INF)
        pr = jax.nn.softmax(s, axis=-1).astype(ckv.dtype)
        return jnp.einsum('bhqk,bkl->bqhl', pr, ckv)

    return _map_qblocks(block, (q_lat, q_pe), q_pos)


def _diff_attention(q, k, v, lam, table, q_pos, k_pos):
    scale = DH_D ** -0.5

    def block(args):
        qb, qpos = args
        s = jnp.einsum('bqhcd,bkhcd->bchqk', qb, k, preferred_element_type=F32) * scale
        s = s + _t5_bias(table, qpos, k_pos)[None, None]
        s = jnp.where((k_pos[None, :] <= qpos[:, None])[None, None, None], s, NEG_INF)
        pr = jax.nn.softmax(s, axis=-1)
        w = (pr[:, 0] - lam * pr[:, 1]).astype(v.dtype)
        return jnp.einsum('bhqk,bkhv->bqhv', w, v)

    return _map_qblocks(block, (q,), q_pos)


def _rglru(xb, gate, conv_buf, h0, conv_w, conv_b, wa, ba, wx, bx, lam):
    b, t, w = xb.shape
    xpad = jnp.concatenate([conv_buf.astype(xb.dtype), xb], axis=1)
    conv = conv_b
    for j in range(CONV_W):
        conv = conv + xpad[:, j:j + t] * conv_w[j]
    xr = conv.reshape(b, t, LRU_BLOCKS, LRU_BW)
    r = jax.nn.sigmoid(jnp.einsum('btnc,ncd->btnd', xr, wa).reshape(b, t, w) + ba)
    ig = jax.nn.sigmoid(jnp.einsum('btnc,ncd->btnd', xr, wx).reshape(b, t, w) + bx)
    log_a = -LRU_C * r.astype(F32) * jax.nn.softplus(-lam.astype(F32))
    a = jnp.exp(log_a)
    u = jnp.sqrt(-jnp.expm1(2.0 * log_a)) * (ig * conv).astype(F32)

    def step(hc, au):
        hn = au[0] * hc + au[1]
        return hn, hn

    h_last, hs = lax.scan(step, h0.astype(F32), (jnp.moveaxis(a, 1, 0), jnp.moveaxis(u, 1, 0)))
    y = jnp.moveaxis(hs, 0, 1).astype(xb.dtype) * jax.nn.gelu(gate)
    return y, xpad[:, t:], h_last.astype(xb.dtype)


def _retention(q, k, v, s0):
    b, t, h, _ = q.shape
    c = min(RET_CHUNK, t)
    n = t // c
    lg = jnp.log1p(-(2.0 ** (-5.0 - jnp.arange(h, dtype=F32))))
    idx = jnp.arange(c, dtype=F32)
    diff = idx[:, None] - idx[None, :]
    intra = jnp.where(diff >= 0, jnp.exp(jnp.maximum(diff, 0.0)[None] * lg[:, None, None]), 0.0)
    q_dec = jnp.exp((idx[:, None] + 1.0) * lg[None, :])
    k_dec = jnp.exp((c - 1.0 - idx)[:, None] * lg[None, :])
    chunk_dec = jnp.exp(c * lg)

    def chunk(s, qkv):
        qc, kc, vc = qkv
        att = jnp.einsum('bqhd,bkhd->bhqk', qc, kc) * intra[None]
        o = (jnp.einsum('bhqk,bkhv->bqhv', att, vc)
             + jnp.einsum('bqhd,bhdv->bqhv', qc, s) * q_dec[None, :, :, None])
        s = chunk_dec[None, :, None, None] * s + jnp.einsum('bkhd,bkhv->bhdv', kc * k_dec[None, :, :, None], vc)
        return s, o

    def blocks(a):
        return jnp.moveaxis(a.astype(F32).reshape(b, n, c, h, a.shape[-1]), 1, 0)

    s_last, o = lax.scan(chunk, s0.astype(F32), (blocks(q), blocks(k), blocks(v)))
    o = jnp.moveaxis(o, 0, 1).reshape(b, t, h, v.shape[-1])
    return o.astype(q.dtype), s_last


def _gather_pages(pool, page_table):
    g = jnp.take(pool, page_table, axis=0)
    return g.reshape((g.shape[0], g.shape[1] * g.shape[2]) + g.shape[3:])


def _layer(i, h, p, pos, past, conv_buf, lru_h, ret_s, W):
    b, t, _ = h.shape
    ng = W['norm_g'][i]
    h = h + 0.5 * _rmsnorm(_swiglu(_rmsnorm(h, ng[0]), W['w_ffn_gu'][i, 0], W['w_ffn_down'][i, 0]), ng[1])
    u = _rmsnorm(h, ng[2])
    z = u @ W['w_in'][i]
    (cq, ckv, kpe, lgate, lx, rq, rk, rv, rg, dq, dk, dv, gates) = _split(z, IN_SIZES)

    cq = _rmsnorm(cq, W['g_q'][i])
    q = (cq @ W['w_uq'][i]).reshape(b, t, H_A, NOPE_A + ROPE_A)
    q_nope = q[..., :NOPE_A]
    q_pe = _rope(q[..., NOPE_A:], pos)
    ckv = _rmsnorm(ckv, W['g_kv'][i])
    kpe = _rope(kpe, pos)
    q_lat = jnp.einsum('bthn,lhn->bthl', q_nope, W['w_uk'][i])
    if past is None:
        ckv_all, kpe_all, k_pos = ckv, kpe, pos
    else:
        ckv_all = jnp.concatenate([past[0].astype(ckv.dtype), ckv], axis=1)
        kpe_all = jnp.concatenate([past[1].astype(kpe.dtype), kpe], axis=1)
        k_pos = jnp.concatenate([past[4], pos])
    o_lat = _mla_attention(q_lat, q_pe, ckv_all, kpe_all, pos, k_pos)
    o_a = jnp.einsum('bthl,lhv->bthv', o_lat, W['w_uv'][i]).reshape(b, t, H_A * V_A)

    o_b, conv_new, lru_new = _rglru(lx, lgate, conv_buf, lru_h, W['conv_w'][i], W['conv_b'][i],
                                    W['lru_wa'][i], W['lru_ba'][i], W['lru_wx'][i], W['lru_bx'][i],
                                    W['lru_lambda'][i])

    rq = _rope(rq.reshape(b, t, H_C, DK_C), pos)
    rk = _rope(rk.reshape(b, t, H_C, DK_C), pos) * (DK_C ** -0.5)
    o_c, ret_new = _retention(rq, rk, rv.reshape(b, t, H_C, DV_C), ret_s)
    o_c = jax.nn.silu(rg) * _group_norm(o_c, W['ret_gn_g'][i])

    lam_init = 0.8 - 0.6 * math.exp(-0.3 * i)
    dl = W['diff_lambda'][i].astype(F32)
    lam = jnp.exp(jnp.sum(dl[0] * dl[1])) - jnp.exp(jnp.sum(dl[2] * dl[3])) + lam_init
    dq = dq.reshape(b, t, H_D, 2, DH_D)
    dk_rows = dk.reshape(b, t, H_D, 2 * DH_D)
    dv_rows = dv.reshape(b, t, H_D, DV_D)
    if past is None:
        dk_all, dv_all = dk_rows, dv_rows
    else:
        dk_all = jnp.concatenate([past[2].astype(dk_rows.dtype), dk_rows], axis=1)
        dv_all = jnp.concatenate([past[3].astype(dv_rows.dtype), dv_rows], axis=1)
    o_d = _diff_attention(dq, dk_all.reshape(dk_all.shape[:3] + (2, DH_D)), dv_all, lam,
                          W['t5_bias'], pos, k_pos)
    o_d = (_rmsnorm(o_d, W['diff_subln_g'][i]) * (1.0 - lam_init)).reshape(b, t, H_D * DV_D)

    br = jnp.stack([o_a, o_b, o_c, o_d], axis=2)
    proj = jnp.einsum('btnw,nwd->btnd', br, W['w_branch'][i])
    gt = jax.nn.sigmoid(gates.reshape(b, t, N_BRANCH, D_MODEL).astype(F32)).astype(proj.dtype)
    mix = jnp.sum(gt * proj, axis=2) @ W['w_out'][i]
    h = h + _rmsnorm(mix, ng[3])
    h = h + 0.5 * _rmsnorm(_swiglu(_rmsnorm(h, ng[4]), W['w_ffn_gu'][i, 1], W['w_ffn_down'][i, 1]), ng[5])
    e = jax.nn.sigmoid(_rmsnorm(h, ng[6]) @ W['w_ple_gate'][i]) * (p @ W['w_ple_in'][i])
    h = h + _rmsnorm(e, ng[7])
    return h, (ckv, kpe, dk_rows, dv_rows, lru_new, conv_new, ret_new.astype(h.dtype))


def setup_inputs(seed: int = 0) -> dict:
    key = jax.random.key(seed)
    k = jax.random.split(key, 48)

    def nrm(j, shape, scale):
        return jax.random.normal(k[j], shape, F32) * scale

    n_pages = PAST_LEN // PAGE_SIZE
    n_used = DEC_BATCH * n_pages
    n_pool = n_used + (n_used + 3) // 4
    page_table = jax.random.permutation(k[0], n_pool)[:n_used].reshape(DEC_BATCH, n_pages).astype(jnp.int32)
    a_c = jax.random.uniform(k[1], (DEPTH, LRU_W), F32, 0.9, 0.999)
    a = a_c ** (1.0 / LRU_C)
    lru_lambda = jnp.log(a) - jnp.log1p(-a)
    return {
        'x_prompt': nrm(2, (BATCH, SEQ, D_MODEL), 1.0),
        'x_sample': nrm(3, (DEC_BATCH, DEC_SEQ, D_MODEL), 1.0),
        'cache_mla_ckv': nrm(4, (DEPTH, n_pool, PAGE_SIZE, KV_LORA), 1.0),
        'cache_mla_kpe': nrm(5, (DEPTH, n_pool, PAGE_SIZE, ROPE_A), 1.0),
        'cache_diff_k': nrm(6, (DEPTH, n_pool, PAGE_SIZE, H_D, 2 * DH_D), 1.0),
        'cache_diff_v': nrm(7, (DEPTH, n_pool, PAGE_SIZE, H_D, DV_D), 1.0),
        'state_lru_h': nrm(8, (DEPTH, DEC_BATCH, LRU_W), 0.5),
        'state_lru_conv': nrm(9, (DEPTH, DEC_BATCH, CONV_W - 1, LRU_W), 1.0),
        'state_ret': nrm(10, (DEPTH, DEC_BATCH, H_C, DK_C, DV_C), 1.0),
        'page_table': page_table,
        'p_prompt': nrm(11, (DEPTH, BATCH, SEQ, PLE_DIM), 1.0),
        'p_sample': nrm(12, (DEPTH, DEC_BATCH, DEC_SEQ, PLE_DIM), 1.0),
        't5_bias': nrm(13, (T5_BUCKETS, H_D), 0.5),
        'norm_g': 1.0 + nrm(14, (DEPTH, N_NORMS, D_MODEL), 0.02),
        'w_ffn_gu': nrm(15, (DEPTH, 2, D_MODEL, 2 * FF_DIM), D_MODEL ** -0.5),
        'w_ffn_down': nrm(16, (DEPTH, 2, FF_DIM, D_MODEL), FF_DIM ** -0.5),
        'w_in': nrm(17, (DEPTH, D_MODEL, IN_COLS), D_MODEL ** -0.5),
        'w_uq': nrm(18, (DEPTH, Q_LORA, H_A * (NOPE_A + ROPE_A)), Q_LORA ** -0.5),
        'g_q': 1.0 + nrm(19, (DEPTH, Q_LORA), 0.02),
        'g_kv': 1.0 + nrm(20, (DEPTH, KV_LORA), 0.02),
        'w_uk': nrm(21, (DEPTH, KV_LORA, H_A, NOPE_A), KV_LORA ** -0.5),
        'w_uv': nrm(22, (DEPTH, KV_LORA, H_A, V_A), KV_LORA ** -0.5),
        'conv_w': nrm(23, (DEPTH, CONV_W, LRU_W), CONV_W ** -0.5),
        'conv_b': nrm(24, (DEPTH, LRU_W), 0.01),
        'lru_wa': nrm(25, (DEPTH, LRU_BLOCKS, LRU_BW, LRU_BW), LRU_BW ** -0.5),
        'lru_ba': nrm(26, (DEPTH, LRU_W), 0.01),
        'lru_wx': nrm(27, (DEPTH, LRU_BLOCKS, LRU_BW, LRU_BW), LRU_BW ** -0.5),
        'lru_bx': nrm(28, (DEPTH, LRU_W), 0.01),
        'lru_lambda': lru_lambda,
        'ret_gn_g': 1.0 + nrm(29, (DEPTH, H_C * DV_C), 0.02),
        'diff_lambda': nrm(30, (DEPTH, 4, DH_D), 0.1),
        'diff_subln_g': 1.0 + nrm(31, (DEPTH, DV_D), 0.02),
        'w_branch': nrm(32, (DEPTH, N_BRANCH, BRANCH_W, D_MODEL), BRANCH_W ** -0.5),
        'w_out': nrm(33, (DEPTH, D_MODEL, D_MODEL), D_MODEL ** -0.5),
        'w_ple_in': nrm(34, (DEPTH, PLE_DIM, D_MODEL), PLE_DIM ** -0.5),
        'w_ple_gate': nrm(35, (DEPTH, D_MODEL, D_MODEL), D_MODEL ** -0.5),
    }


def reference(x_prompt, x_sample, cache_mla_ckv, cache_mla_kpe, cache_diff_k, cache_diff_v,
              state_lru_h, state_lru_conv, state_ret, page_table, p_prompt, p_sample,
              t5_bias, norm_g, w_ffn_gu, w_ffn_down, w_in, w_uq, g_q, g_kv, w_uk, w_uv,
              conv_w, conv_b, lru_wa, lru_ba, lru_wx, lru_bx, lru_lambda, ret_gn_g,
              diff_lambda, diff_subln_g, w_branch, w_out, w_ple_in, w_ple_gate):
    W = {'t5_bias': t5_bias, 'norm_g': norm_g, 'w_ffn_gu': w_ffn_gu, 'w_ffn_down': w_ffn_down,
         'w_in': w_in, 'w_uq': w_uq, 'g_q': g_q, 'g_kv': g_kv, 'w_uk': w_uk, 'w_uv': w_uv,
         'conv_w': conv_w, 'conv_b': conv_b, 'lru_wa': lru_wa, 'lru_ba': lru_ba,
         'lru_wx': lru_wx, 'lru_bx': lru_bx, 'lru_lambda': lru_lambda, 'ret_gn_g': ret_gn_g,
         'diff_lambda': diff_lambda, 'diff_subln_g': diff_subln_g, 'w_branch': w_branch,
         'w_out': w_out, 'w_ple_in': w_ple_in, 'w_ple_gate': w_ple_gate}
    b_p, seq, _ = x_prompt.shape
    dec_seq = x_sample.shape[1]
    past_len = page_table.shape[1] * cache_mla_ckv.shape[2]
    pos_p = jnp.arange(seq, dtype=jnp.int32)
    pos_s = past_len + jnp.arange(dec_seq, dtype=jnp.int32)
    past_pos = jnp.arange(past_len, dtype=jnp.int32)
    dt = x_prompt.dtype
    hp, hs = x_prompt, x_sample
    new_p = [[] for _ in range(7)]
    new_s = [[] for _ in range(7)]
    for i in range(DEPTH):
        hp, st_p = _layer(i, hp, p_prompt[i], pos_p, None,
                          jnp.zeros((b_p, CONV_W - 1, LRU_W), dt), jnp.zeros((b_p, LRU_W), dt),
                          jnp.zeros((b_p, H_C, DK_C, DV_C), dt), W)
        past = (_gather_pages(cache_mla_ckv[i], page_table), _gather_pages(cache_mla_kpe[i], page_table),
                _gather_pages(cache_diff_k[i], page_table), _gather_pages(cache_diff_v[i], page_table),
                past_pos)
        hs, st_s = _layer(i, hs, p_sample[i], pos_s, past, state_lru_conv[i], state_lru_h[i],
                          state_ret[i], W)
        for j in range(7):
            new_p[j].append(st_p[j])
            new_s[j].append(st_s[j])
    sp = [jnp.stack(a) for a in new_p]
    ss = [jnp.stack(a) for a in new_s]
    return (hp, hs, sp[0], sp[1], sp[2], sp[3], sp[4], sp[5], sp[6],
            ss[0], ss[1], ss[2], ss[3], ss[4], ss[5], ss[6])
```

```python
import functools
import math

import numpy as np
import jax
import jax.numpy as jnp
from jax import lax
from jax.experimental import pallas as pl
from jax.experimental.pallas import tpu as pltpu

F32 = jnp.float32
BF16 = jnp.bfloat16

N_BRANCH = 4
H_A, Q_LORA, KV_LORA, NOPE_A, ROPE_A, V_A = 4, 256, 128, 64, 32, 64
ROPE_THETA = 10000.0
LRU_W, LRU_BLOCKS, CONV_W, LRU_C = 256, 4, 4, 8.0
H_C, DK_C, DV_C, RET_CHUNK = 4, 64, 64, 128
H_D, DH_D, DV_D = 4, 32, 64
T5_BUCKETS, T5_MAX_DIST = 32, 128
EPS = 1e-6
NEG_INF = -1e30
BW = 256

VMEM_LIMIT = 56 * 1024 * 1024
LANES = 128

_NT = (((1,), (1,)), ((), ()))
_TN = (((0,), (0,)), ((), ()))


def _cp(*sem):
    return pltpu.CompilerParams(dimension_semantics=sem, vmem_limit_bytes=VMEM_LIMIT)


def _rms(x, g):
    return x * lax.rsqrt(jnp.mean(x * x, axis=-1, keepdims=True) + EPS) * g


def _dot(a, b):
    return jnp.dot(a, b, preferred_element_type=F32)


def _dot_nt(a, b):
    return lax.dot_general(a, b, _NT, preferred_element_type=F32)


def _sigmoid(x):
    return 1.0 / (1.0 + jnp.exp(-x))


def _ffn_kernel(x_ref, g_ref, wg_ref, wu_ref, wd_ref, o_ref, xn_ref, acc_ref):
    j = pl.program_id(1)

    @pl.when(j == 0)
    def _():
        xn_ref[...] = _rms(x_ref[...], g_ref[0:1, :]).astype(BF16)
        acc_ref[...] = jnp.zeros_like(acc_ref)

    xn = xn_ref[...]
    g = _dot(xn, wg_ref[...])
    u = _dot(xn, wu_ref[...])
    a = (g * _sigmoid(g) * u).astype(BF16)
    acc_ref[...] += _dot(a, wd_ref[...])

    @pl.when(j == pl.num_programs(1) - 1)
    def _():
        o_ref[...] = x_ref[...] + 0.5 * _rms(acc_ref[...], g_ref[1:2, :])


def _ffn(h, g2, wgu, wdn, i, k, tm, fc):
    n, d = h.shape
    ff = wdn.shape[2]
    nf = ff // fc
    return pl.pallas_call(
        _ffn_kernel,
        grid=(n // tm, nf),
        in_specs=[
            pl.BlockSpec((tm, d), lambda t, j: (t, 0)),
            pl.BlockSpec((2, d), lambda t, j: (0, 0)),
            pl.BlockSpec((None, None, d, fc), lambda t, j: (i, k, 0, j)),
            pl.BlockSpec((None, None, d, fc), lambda t, j: (i, k, 0, nf + j)),
            pl.BlockSpec((None, None, fc, d), lambda t, j: (i, k, j, 0)),
        ],
        out_specs=pl.BlockSpec((tm, d), lambda t, j: (t, 0)),
        out_shape=jax.ShapeDtypeStruct((n, d), F32),
        scratch_shapes=[pltpu.VMEM((tm, d), BF16), pltpu.VMEM((tm, d), F32)],
        compiler_params=_cp("parallel", "arbitrary"),
        name="ffn",
    )(h, g2, wgu, wgu, wdn)


_C_CQ, _C_CKV, _C_KPE, _C_KPES, _C_LG, _C_LX = 0, 256, 384, 512, 640, 896
_C_RQ, _C_RK, _C_RV, _C_RG, _C_DQ, _C_DK, _C_DV, _C_RQS, _C_RKS = 1152, 1408, 1664, 1920, 2176, 2432, 2688, 2944, 3200
_W1_COLS = 3456


def _swap_halves(width, group):
    idx = np.arange(width)
    half = group // 2
    return np.where(idx % group < half, idx + half, idx - half)


def _w1_columns():
    o_cq, o_ckv, o_kpe, o_lg, o_lx = 0, 256, 384, 416, 672
    o_rq, o_rk, o_rv, o_rg, o_dq, o_dk, o_dv = 928, 1184, 1440, 1696, 1952, 2208, 2464
    r = np.arange
    kpe = o_kpe + np.tile(r(ROPE_A), 4)
    kpe_s = o_kpe + np.tile(_swap_halves(ROPE_A, ROPE_A), 4)
    cols = np.concatenate([
        o_cq + r(256), o_ckv + r(128), kpe, kpe_s, o_lg + r(256), o_lx + r(256),
        o_rq + r(256), o_rk + r(256), o_rv + r(256), o_rg + r(256),
        o_dq + r(256), o_dk + r(256), o_dv + r(256),
        o_rq + _swap_halves(256, DK_C), o_rk + _swap_halves(256, DK_C)])
    assert cols.shape[0] == _W1_COLS
    return cols


def _inproj_kernel(x_ref, g_ref, w1_ref, tab_ref, gq_ref, gkv_ref, wuq_ref, wuk_ref,
                   ckv_o, kpe_o, kcat_o, qlat_o, qpe_o, lg_o, lx_o, rq_o, rk_o, rv_o, rg_o,
                   dq_o, dk_o, dv_o, dkb_o, dvb_o):
    un = _rms(x_ref[...], g_ref[...]).astype(BF16)
    z = _dot(un, w1_ref[...])
    cosc, sinc = tab_ref[:, 0:256], tab_ref[:, 256:512]
    cosa, sina = tab_ref[:, 512:640], tab_ref[:, 640:768]

    cqn = _rms(z[:, _C_CQ:_C_CQ + 256], gq_ref[...]).astype(BF16)
    q = _dot(cqn, wuq_ref[...])
    qpe_o[...] = (q[:, 256:384] * cosa + q[:, 384:512] * sina).astype(BF16)
    qlat_o[...] = _dot(q[:, 0:256].astype(BF16), wuk_ref[...]).astype(BF16)

    ckvn = _rms(z[:, _C_CKV:_C_CKV + 128], gkv_ref[...])
    kpe4 = z[:, _C_KPE:_C_KPE + 128] * cosa + z[:, _C_KPES:_C_KPES + 128] * sina
    ckv_o[...] = ckvn
    kpe_o[...] = kpe4
    kcat_o[:, 0:128] = ckvn.astype(BF16)
    kcat_o[:, 128:256] = kpe4.astype(BF16)

    lg_o[...] = z[:, _C_LG:_C_LG + 256]
    lx_o[...] = z[:, _C_LX:_C_LX + 256]
    rq_o[...] = (z[:, _C_RQ:_C_RQ + 256] * cosc + z[:, _C_RQS:_C_RQS + 256] * sinc).astype(BF16)
    rk_o[...] = (z[:, _C_RK:_C_RK + 256] * cosc + z[:, _C_RKS:_C_RKS + 256] * sinc) * (DK_C ** -0.5)
    rv_o[...] = z[:, _C_RV:_C_RV + 256].astype(BF16)
    rg_o[...] = z[:, _C_RG:_C_RG + 256]
    dq_o[...] = z[:, _C_DQ:_C_DQ + 256].astype(BF16)
    dk = z[:, _C_DK:_C_DK + 256]
    dv = z[:, _C_DV:_C_DV + 256]
    dk_o[...] = dk
    dv_o[...] = dv
    dkb_o[...] = dk.astype(BF16)
    dvb_o[...] = dv.astype(BF16)


def _inproj(h, g, w1, tab, gq, gkv, wuq, wuk, i, tm, tab_index):
    n, d = h.shape
    widths = [(128, F32), (128, F32), (256, BF16), (512, BF16), (128, BF16), (256, F32), (256, F32),
              (256, BF16), (256, F32), (256, BF16), (256, F32), (256, BF16), (256, F32), (256, F32),
              (256, BF16), (256, BF16)]
    return pl.pallas_call(
        _inproj_kernel,
        grid=(n // tm,),
        in_specs=[
            pl.BlockSpec((tm, d), lambda t: (t, 0)),
            pl.BlockSpec((1, d), lambda t: (0, 0)),
            pl.BlockSpec((None, d, _W1_COLS), lambda t: (i, 0, 0)),
            pl.BlockSpec((tm, 768), lambda t: (tab_index(t), 0)),
            pl.BlockSpec((1, Q_LORA), lambda t: (0, 0)),
            pl.BlockSpec((1, KV_LORA), lambda t: (0, 0)),
            pl.BlockSpec((None, Q_LORA, 512), lambda t: (i, 0, 0)),
            pl.BlockSpec((None, 256, 512), lambda t: (i, 0, 0)),
        ],
        out_specs=[pl.BlockSpec((tm, w), lambda t: (t, 0)) for w, _ in widths],
        out_shape=[jax.ShapeDtypeStruct((n, w), dt) for w, dt in widths],
        compiler_params=_cp("parallel"),
        name="inproj",
    )(h, g, w1, tab, gq, gkv, wuq, wuk)


def _mla_prompt_kernel(qlat_ref, qpe_ref, k_ref, wuv_ref, o_ref, *, tq):
    qi = pl.program_id(1)
    scale = (NOPE_A + ROPE_A) ** -0.5
    qpe = qpe_ref[...].astype(F32)
    lane = lax.broadcasted_iota(jnp.int32, (tq, LANES), 1)
    row = lax.broadcasted_iota(jnp.int32, (tq, tq), 0)
    col = lax.broadcasted_iota(jnp.int32, (tq, tq), 1)
    out = jnp.zeros((tq, BW), F32)
    for h in range(H_A):
        qh = jnp.concatenate(
            [qlat_ref[:, h * KV_LORA:(h + 1) * KV_LORA],
             jnp.where(lane // ROPE_A == h, qpe, 0.0).astype(BF16)], axis=1)

        def step(j, carry, masked, qh=qh):
            m, l, acc = carry
            k = k_ref[pl.ds(pl.multiple_of(j * tq, tq), tq), :]
            s = _dot_nt(qh, k) * scale
            if masked:
                s = jnp.where(col <= row, s, NEG_INF)
            m_new = jnp.maximum(m, jnp.max(s, axis=-1, keepdims=True))
            alpha = jnp.exp(m - m_new)
            p = jnp.exp(s - m_new)
            l = alpha * l + jnp.sum(p, axis=-1, keepdims=True)
            acc = alpha * acc + _dot(p.astype(BF16), k[:, 0:KV_LORA])
            return m_new, l, acc

        init = (jnp.full((tq, 1), NEG_INF, F32), jnp.zeros((tq, 1), F32), jnp.zeros((tq, KV_LORA), F32))
        carry = lax.fori_loop(0, qi, functools.partial(step, masked=False), init)
        _, l, acc = step(qi, carry, True)
        o_lat = acc / l
        out = out + _dot(o_lat.astype(BF16), wuv_ref[h])
    o_ref[...] = out


def _mla_prompt(qlat, qpe, kcat, wuvp, i, b, t, tq):
    nq = t // tq
    return pl.pallas_call(
        functools.partial(_mla_prompt_kernel, tq=tq),
        grid=(b, nq),
        in_specs=[
            pl.BlockSpec((tq, 512), lambda bb, qi: (bb * nq + qi, 0)),
            pl.BlockSpec((tq, 128), lambda bb, qi: (bb * nq + qi, 0)),
            pl.BlockSpec((t, 256), lambda bb, qi: (bb, 0)),
            pl.BlockSpec((None, H_A, KV_LORA, BW), lambda bb, qi: (i, 0, 0, 0)),
        ],
        out_specs=pl.BlockSpec((tq, BW), lambda bb, qi: (bb * nq + qi, 0)),
        out_shape=jax.ShapeDtypeStruct((b * t, BW), F32),
        compiler_params=_cp("parallel", "arbitrary"),
        name="mla_prompt",
    )(qlat, qpe, kcat, wuvp)


def _diff_lambda(dl, lam_init):
    a = jnp.sum(dl[0:1, :] * dl[1:2, :], axis=-1, keepdims=True)
    b = jnp.sum(dl[2:3, :] * dl[3:4, :], axis=-1, keepdims=True)
    return jnp.exp(a) - jnp.exp(b) + lam_init


def _diff_prompt_kernel(dq_ref, k_ref, v_ref, bias_ref, dl_ref, sg_ref, o_ref, q8_ref, *, tq, lam_init):
    qi = pl.program_id(1)
    scale = DH_D ** -0.5
    lam = _diff_lambda(dl_ref[...], lam_init)
    q = dq_ref[...].astype(F32)
    lane = lax.broadcasted_iota(jnp.int32, (tq, BW), 1)
    for hc in range(2 * H_D):
        q8_ref[hc * tq:(hc + 1) * tq, :] = jnp.where(lane // DH_D == hc, q, 0.0).astype(BF16)
    row = lax.broadcasted_iota(jnp.int32, (tq, tq), 0)
    col = lax.broadcasted_iota(jnp.int32, (tq, tq), 1)

    def step(j, carry, masked):
        ms, ls, accs = carry
        k = k_ref[pl.ds(pl.multiple_of(j * tq, tq), tq), :]
        v = v_ref[pl.ds(pl.multiple_of(j * tq, tq), tq), :]
        s_all = _dot_nt(q8_ref[...], k)
        ms2, ls2, accs2 = [], [], []
        for hc in range(2 * H_D):
            h = hc // 2
            s = s_all[hc * tq:(hc + 1) * tq, :] * scale + bias_ref[qi - j, h]
            if masked:
                s = jnp.where(col <= row, s, NEG_INF)
            m_new = jnp.maximum(ms[hc], jnp.max(s, axis=-1, keepdims=True))
            alpha = jnp.exp(ms[hc] - m_new)
            p = jnp.exp(s - m_new)
            ls2.append(alpha * ls[hc] + jnp.sum(p, axis=-1, keepdims=True))
            vh = v[:, (h // 2) * LANES:(h // 2 + 1) * LANES]
            accs2.append(alpha * accs[hc] + _dot(p.astype(BF16), vh))
            ms2.append(m_new)
        return tuple(ms2), tuple(ls2), tuple(accs2)

    n = 2 * H_D
    init = (tuple(jnp.full((tq, 1), NEG_INF, F32) for _ in range(n)),
            tuple(jnp.zeros((tq, 1), F32) for _ in range(n)),
            tuple(jnp.zeros((tq, LANES), F32) for _ in range(n)))
    carry = lax.fori_loop(0, qi, functools.partial(step, masked=False), init)
    _, ls, accs = step(qi, carry, True)
    sg = sg_ref[...]
    for h in range(H_D):
        o = accs[2 * h] / ls[2 * h] - lam * (accs[2 * h + 1] / ls[2 * h + 1])
        oh = o[:, (h % 2) * DV_D:(h % 2 + 1) * DV_D]
        o_ref[:, h * DV_D:(h + 1) * DV_D] = _rms(oh, sg) * (1.0 - lam_init)


def _diff_prompt(dq, dkb, dvb, bias, dl, sg, i, b, t, tq, lam_init):
    nq = t // tq
    return pl.pallas_call(
        functools.partial(_diff_prompt_kernel, tq=tq, lam_init=lam_init),
        grid=(b, nq),
        in_specs=[
            pl.BlockSpec((tq, BW), lambda bb, qi: (bb * nq + qi, 0)),
            pl.BlockSpec((t, BW), lambda bb, qi: (bb, 0)),
            pl.BlockSpec((t, BW), lambda bb, qi: (bb, 0)),
            pl.BlockSpec((nq, H_D, tq, tq), lambda bb, qi: (0, 0, 0, 0)),
            pl.BlockSpec((None, 4, DH_D), lambda bb, qi: (i, 0, 0)),
            pl.BlockSpec((1, DV_D), lambda bb, qi: (0, 0)),
        ],
        out_specs=pl.BlockSpec((tq, BW), lambda bb, qi: (bb * nq + qi, 0)),
        out_shape=jax.ShapeDtypeStruct((b * t, BW), F32),
        scratch_shapes=[pltpu.VMEM((2 * H_D * tq, BW), BF16)],
        compiler_params=_cp("parallel", "arbitrary"),
        name="diff_prompt",
    )(dq, dkb, dvb, bias, dl, sg)


def _ret_log_gamma(h):
    return math.log1p(-(2.0 ** (-5.0 - h)))


def _lane_heads(shape, axis, width):
    return lax.broadcasted_iota(jnp.int32, shape, axis) // width


def _per_head(idx, vals):
    out = jnp.full(idx.shape, vals[-1], F32)
    for h in range(len(vals) - 2, -1, -1):
        out = jnp.where(idx == h, vals[h], out)
    return out


def _group_norm_lanes(o, headl, n_heads, width):
    mu = jnp.zeros_like(o)
    for h in range(n_heads):
        mh = headl == h
        s = jnp.sum(jnp.where(mh, o, 0.0), axis=-1, keepdims=True) * (1.0 / width)
        mu = jnp.where(mh, s, mu)
    d = o - mu
    var = jnp.zeros_like(o)
    for h in range(n_heads):
        mh = headl == h
        s = jnp.sum(jnp.where(mh, d * d, 0.0), axis=-1, keepdims=True) * (1.0 / width)
        var = jnp.where(mh, s, var)
    return d * lax.rsqrt(var + EPS)


def _ret_prompt_kernel(q_ref, k_ref, v_ref, g_ref, gn_ref, o_ref, s_ref, s_scr, *, n_chunks):
    c = RET_CHUNK
    lgs = [_ret_log_gamma(h) for h in range(H_C)]
    headl = _lane_heads((c, BW), 1, DK_C)
    lgl = _per_head(headl, lgs)
    idx = lax.broadcasted_iota(jnp.int32, (c, BW), 0).astype(F32)
    qdec = jnp.exp((idx + 1.0) * lgl)
    kdec = jnp.exp((c - 1.0 - idx) * lgl)
    cdec = jnp.exp(float(c) * _per_head(_lane_heads((BW, BW), 0, DK_C), lgs))
    ri = lax.broadcasted_iota(jnp.int32, (c, c), 0)
    ci = lax.broadcasted_iota(jnp.int32, (c, c), 1)
    dif = (ri - ci).astype(F32)
    intra = [jnp.where(dif >= 0, jnp.exp(jnp.maximum(dif, 0.0) * lgs[h]), 0.0) for h in range(H_C)]
    bd = _lane_heads((BW, BW), 0, DK_C) == _lane_heads((BW, BW), 1, DV_C)
    s_scr[...] = jnp.zeros_like(s_scr)

    def chunk(ic, _):
        rows = pl.ds(pl.multiple_of(ic * c, c), c)
        q = q_ref[rows, :]
        k = k_ref[rows, :]
        v = v_ref[rows, :]
        s = s_scr[...]
        o = _dot(q, s.astype(BF16)) * qdec
        for h in range(H_C):
            mh = headl == h
            att = _dot_nt(q, jnp.where(mh, k, 0.0).astype(BF16)) * intra[h]
            o = o + jnp.where(mh, _dot(att.astype(BF16), v), 0.0)
        kd = (k * kdec).astype(BF16)
        upd = lax.dot_general(kd, v, _TN, preferred_element_type=F32)
        s_scr[...] = cdec * s + jnp.where(bd, upd, 0.0)
        y = _group_norm_lanes(o, headl, H_C, DV_C) * gn_ref[...]
        g = g_ref[rows, :]
        o_ref[rows, :] = g * _sigmoid(g) * y
        return 0

    lax.fori_loop(0, n_chunks, chunk, 0)
    s_ref[...] = s_scr[...]


def _ret_prompt(rq, rk, rv, rg, gn, i, b, t):
    return pl.pallas_call(
        functools.partial(_ret_prompt_kernel, n_chunks=t // RET_CHUNK),
        grid=(b,),
        in_specs=[pl.BlockSpec((t, BW), lambda bb: (bb, 0))] * 4 + [pl.BlockSpec((1, BW), lambda bb: (0, 0))],
        out_specs=[pl.BlockSpec((t, BW), lambda bb: (bb, 0)), pl.BlockSpec((None, BW, BW), lambda bb: (bb, 0, 0))],
        out_shape=[jax.ShapeDtypeStruct((b * t, BW), F32), jax.ShapeDtypeStruct((b, BW, BW), F32)],
        scratch_shapes=[pltpu.VMEM((BW, BW), F32)],
        compiler_params=_cp("parallel"),
        name="ret_prompt",
    )(rq, rk, rv, rg, gn)


def _lru_gates(conv, wa, wx, vec):
    cb = conv.astype(BF16)
    r = _sigmoid(_dot(cb, wa) + vec[1:2, :])
    ig = _sigmoid(_dot(cb, wx) + vec[2:3, :])
    nl = -vec[3:4, :]
    sp = jnp.maximum(nl, 0.0) + jnp.log1p(jnp.exp(-jnp.abs(nl)))
    log_a = -LRU_C * r * sp
    a = jnp.exp(log_a)
    u = jnp.sqrt(-jnp.tanh(log_a) * (a * a + 1.0)) * (ig * conv)
    return a, u


def _gelu(x):
    return 0.5 * x * (1.0 + jnp.tanh(math.sqrt(2.0 / math.pi) * (x + 0.044715 * (x * x * x))))


def _lru_prompt_kernel(x_ref, g_ref, cw_ref, vec_ref, wa_ref, wx_ref, y_ref, hl_ref,
                       xp, a_s, u_s, h_s, *, rc):
    ic = pl.program_id(1)
    pad = 8

    @pl.when(ic == 0)
    def _():
        xp[0:pad, :] = jnp.zeros((pad, LRU_W), F32)
        h_s[...] = jnp.zeros_like(h_s)

    xp[pad:pad + rc, :] = x_ref[...]
    vec = vec_ref[...]
    conv = vec[0:1, :] + cw_ref[3:4, :] * xp[pad:pad + rc, :]
    for j in range(CONV_W - 1):
        k = CONV_W - 1 - j
        conv = conv + cw_ref[j:j + 1, :] * xp[pad - k:pad - k + rc, :]
    xp[0:pad, :] = xp[rc:rc + pad, :]

    a, u = _lru_gates(conv, wa_ref[...], wx_ref[...], vec)
    a_s[0:pad, :] = jnp.ones((pad, LRU_W), F32)
    u_s[0:pad, :] = jnp.zeros((pad, LRU_W), F32)
    a_s[pad:pad + rc, :] = a
    u_s[pad:pad + rc, :] = u
    tmod = lax.broadcasted_iota(jnp.int32, (rc, LRU_W), 0) % pad
    for d in (1, 2, 4):
        a_cur, u_cur = a_s[pad:pad + rc, :], u_s[pad:pad + rc, :]
        a_sh, u_sh = a_s[pad - d:pad - d + rc, :], u_s[pad - d:pad - d + rc, :]
        m = tmod >= d
        a_s[pad:pad + rc, :] = jnp.where(m, a_cur * a_sh, a_cur)
        u_s[pad:pad + rc, :] = jnp.where(m, a_cur * u_sh + u_cur, u_cur)
    h = h_s[...]
    for blk in range(rc // pad):
        lo = pad + blk * pad
        hs = a_s[lo:lo + pad, :] * h + u_s[lo:lo + pad, :]
        u_s[lo:lo + pad, :] = hs
        h = hs[pad - 1:pad, :]
    h_s[...] = h
    hl_ref[...] = h
    y_ref[...] = u_s[pad:pad + rc, :] * _gelu(g_ref[...])


def _lru_prompt(lx, lgate, cw, vec, wa, wx, i, b, t, rc):
    nc = t // rc
    return pl.pallas_call(
        functools.partial(_lru_prompt_kernel, rc=rc),
        grid=(b, nc),
        in_specs=[
            pl.BlockSpec((rc, LRU_W), lambda bb, c: (bb * nc + c, 0)),
            pl.BlockSpec((rc, LRU_W), lambda bb, c: (bb * nc + c, 0)),
            pl.BlockSpec((None, CONV_W, LRU_W), lambda bb, c: (i, 0, 0)),
            pl.BlockSpec((None, 4, LRU_W), lambda bb, c: (i, 0, 0)),
            pl.BlockSpec((None, LRU_W, LRU_W), lambda bb, c: (i, 0, 0)),
            pl.BlockSpec((None, LRU_W, LRU_W), lambda bb, c: (i, 0, 0)),
        ],
        out_specs=[pl.BlockSpec((rc, LRU_W), lambda bb, c: (bb * nc + c, 0)),
                   pl.BlockSpec((None, 1, LRU_W), lambda bb, c: (bb, 0, 0))],
        out_shape=[jax.ShapeDtypeStruct((b * t, LRU_W), F32), jax.ShapeDtypeStruct((b, 1, LRU_W), F32)],
        scratch_shapes=[pltpu.VMEM((rc + 8, LRU_W), F32), pltpu.VMEM((rc + 8, LRU_W), F32),
                        pltpu.VMEM((rc + 8, LRU_W), F32), pltpu.VMEM((1, LRU_W), F32)],
        compiler_params=_cp("parallel", "arbitrary"),
        name="lru_prompt",
    )(lx, lgate, cw, vec, wa, wx)


def _lru_sample_kernel(x_ref, g_ref, cb_ref, h0_ref, cw_ref, vec_ref, wa_ref, wx_ref, y_ref, hl_ref, *, t):
    vec = vec_ref[...]
    xs = [cb_ref[j] for j in range(CONV_W - 1)] + [x_ref[j] for j in range(t)]
    h = h0_ref[...]
    for n in range(t):
        conv = vec[0:1, :]
        for j in range(CONV_W):
            conv = conv + cw_ref[j:j + 1, :] * xs[n + j]
        a, u = _lru_gates(conv, wa_ref[...], wx_ref[...], vec)
        h = a * h + u
        y_ref[n] = h * _gelu(g_ref[n])
    hl_ref[...] = h


def _lru_sample(xs, gs, cb, h0, cw, vec, wa, wx, i):
    t, b, w = xs.shape
    full = lambda *shape: pl.BlockSpec(shape, lambda g: (0,) * len(shape))
    lay = lambda *shape: pl.BlockSpec((None,) + shape, lambda g: (i,) + (0,) * len(shape))
    return pl.pallas_call(
        functools.partial(_lru_sample_kernel, t=t),
        grid=(1,),
        in_specs=[full(t, b, w), full(t, b, w), lay(CONV_W - 1, b, w), lay(b, w),
                  lay(CONV_W, w), lay(4, w), lay(w, w), lay(w, w)],
        out_specs=[full(t, b, w), full(b, w)],
        out_shape=[jax.ShapeDtypeStruct((t, b, w), F32), jax.ShapeDtypeStruct((b, w), F32)],
        compiler_params=_cp("arbitrary"),
        name="lru_sample",
    )(xs, gs, cb, h0, cw, vec, wa, wx)


def _ret_sample_kernel(q_ref, k_ref, v_ref, g_ref, gn_ref, s_ref, o_ref, sn_ref, *, t):
    for h in range(H_C):
        lg = _ret_log_gamma(h)
        lo = h * DK_C
        qs = [q_ref[n, lo:lo + DK_C, :] for n in range(t)]
        ks = [k_ref[n, lo:lo + DK_C, :] for n in range(t)]
        vs = [v_ref[n, lo:lo + DV_C, :] for n in range(t)]
        os_ = []
        for n in range(t):
            o = jnp.zeros_like(vs[0])
            for m in range(n + 1):
                att = jnp.sum(qs[n] * ks[m], axis=0, keepdims=True) * math.exp((n - m) * lg)
                o = o + att * vs[m]
            os_.append(o)
        kdec = [math.exp((t - 1 - m) * lg) for m in range(t)]
        qdec = [math.exp((n + 1) * lg) for n in range(t)]
        cdec = math.exp(t * lg)

        def body(d, carry, h=h, lo=lo, vs=vs, kdec=kdec, cdec=cdec):
            srow = s_ref[h, d]
            new = cdec * srow
            out = []
            for n in range(t):
                qd = q_ref[n, pl.ds(lo + d, 1), :]
                kd = k_ref[n, pl.ds(lo + d, 1), :]
                out.append(carry[n] + qd * srow)
                new = new + (kd * kdec[n]) * vs[n]
            sn_ref[h, d] = new
            return tuple(out)

        cross = lax.fori_loop(0, DK_C, body, tuple(jnp.zeros_like(vs[0]) for _ in range(t)))
        gn = gn_ref[lo:lo + DV_C, :]
        for n in range(t):
            o = os_[n] + cross[n] * qdec[n]
            mu = jnp.mean(o, axis=0, keepdims=True)
            d = o - mu
            var = jnp.mean(d * d, axis=0, keepdims=True)
            y = d * lax.rsqrt(var + EPS) * gn
            g = g_ref[n, lo:lo + DV_C, :]
            o_ref[n, lo:lo + DV_C, :] = g * _sigmoid(g) * y


def _ret_sample(qT, kT, vT, gT, gnb, state, i):
    t, w, b = qT.shape
    full = lambda *shape: pl.BlockSpec(shape, lambda g: (0,) * len(shape))
    return pl.pallas_call(
        functools.partial(_ret_sample_kernel, t=t),
        grid=(1,),
        in_specs=[full(t, w, b)] * 4 + [full(w, b),
                  pl.BlockSpec((None, H_C, DK_C, DV_C, b), lambda g: (i, 0, 0, 0, 0))],
        out_specs=[full(t, w, b), full(H_C, DK_C, DV_C, b)],
        out_shape=[jax.ShapeDtypeStruct((t, w, b), F32), jax.ShapeDtypeStruct((H_C, DK_C, DV_C, b), F32)],
        compiler_params=_cp("arbitrary"),
        name="ret_sample",
    )(qT, kT, vT, gT, gnb, state)


def _page_copies(pt_ref, hbm, buf, sem, layer, b, slot, n_pages):
    def at(j):
        return pltpu.make_async_copy(hbm.at[layer, pt_ref[b, j]], buf.at[slot, j], sem.at[slot])
    return at


def _start_pages(at, n_pages):
    def body(j, _):
        at(j).start()
        return 0
    lax.fori_loop(0, n_pages, body, 0)


def _wait_pages(at, n_pages):
    def body(j, _):
        at(j).wait()
        return 0
    lax.fori_loop(0, n_pages, body, 0)


_PAGE_UNROLL = 8


def _cols(j, page):
    if isinstance(j, int):
        return slice(j * page, (j + 1) * page)
    return pl.ds(pl.multiple_of(j * page, page), page)


def _mla_sample_kernel(pt_ref, ql_ref, qp_ref, ckvt_ref, kpet_ref, ckv_hbm, kpe_hbm, wuv_ref, o_ref,
                       ckv_buf, kpe_buf, sem_c, sem_k, s_scr, *, layer, n_pages, page):
    b = pl.program_id(0)
    nb = pl.num_programs(0)
    slot = b % 2
    scale = (NOPE_A + ROPE_A) ** -0.5

    def copies(bb, sl):
        return (_page_copies(pt_ref, ckv_hbm, ckv_buf, sem_c, layer, bb, sl, n_pages),
                _page_copies(pt_ref, kpe_hbm, kpe_buf, sem_k, layer, bb, sl, n_pages))

    @pl.when(b == 0)
    def _():
        for at in copies(0, 0):
            _start_pages(at, n_pages)

    @pl.when(b + 1 < nb)
    def _():
        for at in copies(b + 1, 1 - slot):
            _start_pages(at, n_pages)

    for at in copies(b, slot):
        _wait_pages(at, n_pages)

    ql = ql_ref[0]
    qp = qp_ref[0]
    rows = ql.shape[0]

    def score(ck, kp):
        return (_dot_nt(ql, ck.astype(BF16)) + _dot(qp, kp.astype(BF16))) * scale

    def p1(j, m):
        s = score(ckv_buf[slot, j], kpe_buf[slot, j])
        s_scr[:, pl.ds(pl.multiple_of(j * page, page), page)] = s
        return jnp.maximum(m, jnp.max(s, axis=-1, keepdims=True))

    m = lax.fori_loop(0, n_pages, p1, jnp.full((rows, 1), NEG_INF, F32), unroll=_PAGE_UNROLL)
    tok = lax.broadcasted_iota(jnp.int32, (rows, page), 0) % 8
    lane = lax.broadcasted_iota(jnp.int32, (rows, page), 1)
    ckt = ckvt_ref[0]
    st = jnp.where(lane <= tok, score(ckt, kpet_ref[0]), NEG_INF)
    m = jnp.maximum(m, jnp.max(st, axis=-1, keepdims=True))

    def p2(j, carry):
        l, acc = carry
        p = jnp.exp(s_scr[:, pl.ds(pl.multiple_of(j * page, page), page)] - m)
        return l + jnp.sum(p, axis=-1, keepdims=True), acc + _dot(p.astype(BF16), ckv_buf[slot, j].astype(BF16))

    pt_ = jnp.exp(st - m)
    init = (jnp.sum(pt_, axis=-1, keepdims=True), _dot(pt_.astype(BF16), ckt.astype(BF16)))
    l, acc = lax.fori_loop(0, n_pages, p2, init, unroll=_PAGE_UNROLL)
    res = _dot((acc / l).astype(BF16), wuv_ref[...])
    headl = _lane_heads((8, BW), 1, V_A)
    out = jnp.zeros((8, BW), F32)
    for h in range(H_A):
        out = jnp.where(headl == h, res[h * 8:(h + 1) * 8, :], out)
    o_ref[0] = out


def _mla_sample(page_table, ql, qp, ckvt, kpet, cache_ckv, cache_kpeT, wuv_all, i):
    nb, n_pages = page_table.shape
    page = cache_ckv.shape[2]
    grid_spec = pltpu.PrefetchScalarGridSpec(
        num_scalar_prefetch=1,
        grid=(nb,),
        in_specs=[
            pl.BlockSpec((1, 32, KV_LORA), lambda b, pt: (b, 0, 0)),
            pl.BlockSpec((1, 32, ROPE_A), lambda b, pt: (b, 0, 0)),
            pl.BlockSpec((1, page, KV_LORA), lambda b, pt: (b, 0, 0)),
            pl.BlockSpec((1, ROPE_A, page), lambda b, pt: (b, 0, 0)),
            pl.BlockSpec(memory_space=pl.ANY),
            pl.BlockSpec(memory_space=pl.ANY),
            pl.BlockSpec((None, KV_LORA, BW), lambda b, pt: (i, 0, 0)),
        ],
        out_specs=pl.BlockSpec((1, 8, BW), lambda b, pt: (b, 0, 0)),
        scratch_shapes=[
            pltpu.VMEM((2, n_pages, page, KV_LORA), F32),
            pltpu.VMEM((2, n_pages, ROPE_A, page), F32),
            pltpu.SemaphoreType.DMA((2,)),
            pltpu.SemaphoreType.DMA((2,)),
            pltpu.VMEM((32, n_pages * page), F32),
        ],
    )
    return pl.pallas_call(
        functools.partial(_mla_sample_kernel, layer=i, n_pages=n_pages, page=page),
        grid_spec=grid_spec,
        out_shape=jax.ShapeDtypeStruct((nb, 8, BW), F32),
        compiler_params=_cp("arbitrary"),
        name="mla_sample",
    )(page_table, ql, qp, ckvt, kpet, cache_ckv, cache_kpeT, wuv_all)


def _diff_sample_kernel(pt_ref, q_ref, kt_ref, vt_ref, bias_ref, dl_ref, sg_ref, k_hbm, v_hbm, o_ref,
                        k_buf, v_buf, sem_k, sem_v, s_scr, *, layer, n_pages, page, lam_init):
    b = pl.program_id(0)
    nb = pl.num_programs(0)
    slot = b % 2
    scale = DH_D ** -0.5
    lam = _diff_lambda(dl_ref[...], lam_init)

    def copies(bb, sl):
        return (_page_copies(pt_ref, k_hbm, k_buf, sem_k, layer, bb, sl, n_pages),
                _page_copies(pt_ref, v_hbm, v_buf, sem_v, layer, bb, sl, n_pages))

    @pl.when(b == 0)
    def _():
        for at in copies(0, 0):
            _start_pages(at, n_pages)

    @pl.when(b + 1 < nb)
    def _():
        for at in copies(b + 1, 1 - slot):
            _start_pages(at, n_pages)

    for at in copies(b, slot):
        _wait_pages(at, n_pages)

    q8 = q_ref[0]
    lane = lax.broadcasted_iota(jnp.int32, (8, BW), 1)
    q64 = jnp.concatenate(
        [jnp.where(lane // DH_D == 2 * h + c, q8, 0.0) for c in range(2) for h in range(H_D)], axis=0).astype(BF16)
    rows = 64

    def score(kt, j):
        bias = bias_ref[:, _cols(j, page)]
        return _dot(q64, kt.astype(BF16)) * scale + jnp.concatenate([bias, bias], axis=0)

    def p1(j, m):
        s = score(k_buf[slot, j], j)
        s_scr[:, pl.ds(pl.multiple_of(j * page, page), page)] = s
        return jnp.maximum(m, jnp.max(s, axis=-1, keepdims=True))

    m = lax.fori_loop(0, n_pages, p1, jnp.full((rows, 1), NEG_INF, F32), unroll=_PAGE_UNROLL)
    tok = lax.broadcasted_iota(jnp.int32, (rows, page), 0) % 8
    lane_p = lax.broadcasted_iota(jnp.int32, (rows, page), 1)
    st = jnp.where(lane_p <= tok, score(kt_ref[0], n_pages), NEG_INF)
    m = jnp.maximum(m, jnp.max(st, axis=-1, keepdims=True))

    def p2(j, l):
        cols = pl.ds(pl.multiple_of(j * page, page), page)
        p = jnp.exp(s_scr[:, cols] - m)
        s_scr[:, cols] = p
        return l + jnp.sum(p, axis=-1, keepdims=True)

    pt_ = jnp.exp(st - m)
    l = lax.fori_loop(0, n_pages, p2, jnp.sum(pt_, axis=-1, keepdims=True), unroll=_PAGE_UNROLL)
    inv = 1.0 / l
    w1, w2 = inv[0:32, :], lam * inv[32:64, :]

    def weights(p):
        return (p[0:32, :] * w1 - p[32:64, :] * w2).astype(BF16)

    def p3(j, acc):
        cols = pl.ds(pl.multiple_of(j * page, page), page)
        return acc + _dot_nt(weights(s_scr[:, cols]), v_buf[slot, j].astype(BF16))

    acc = lax.fori_loop(0, n_pages, p3, _dot_nt(weights(pt_), vt_ref[0].astype(BF16)), unroll=_PAGE_UNROLL)
    headl = _lane_heads((8, BW), 1, DV_D)
    out = jnp.zeros((8, BW), F32)
    for h in range(H_D):
        out = jnp.where(headl == h, acc[h * 8:(h + 1) * 8, :], out)
    ms = jnp.zeros_like(out)
    for h in range(H_D):
        mh = headl == h
        s = jnp.sum(jnp.where(mh, out * out, 0.0), axis=-1, keepdims=True) * (1.0 / DV_D)
        ms = jnp.where(mh, s, ms)
    o_ref[0] = out * lax.rsqrt(ms + EPS) * sg_ref[...] * (1.0 - lam_init)


def _diff_sample(page_table, q8, kt, vt, bias, dl, sg4, cache_kT, cache_vT, i, lam_init):
    nb, n_pages = page_table.shape
    page = cache_kT.shape[3]
    grid_spec = pltpu.PrefetchScalarGridSpec(
        num_scalar_prefetch=1,
        grid=(nb,),
        in_specs=[
            pl.BlockSpec((1, 8, BW), lambda b, pt: (b, 0, 0)),
            pl.BlockSpec((1, BW, page), lambda b, pt: (b, 0, 0)),
            pl.BlockSpec((1, BW, page), lambda b, pt: (b, 0, 0)),
            pl.BlockSpec((32, (n_pages + 1) * page), lambda b, pt: (0, 0)),
            pl.BlockSpec((None, 4, DH_D), lambda b, pt: (i, 0, 0)),
            pl.BlockSpec((1, BW), lambda b, pt: (0, 0)),
            pl.BlockSpec(memory_space=pl.ANY),
            pl.BlockSpec(memory_space=pl.ANY),
        ],
        out_specs=pl.BlockSpec((1, 8, BW), lambda b, pt: (b, 0, 0)),
        scratch_shapes=[
            pltpu.VMEM((2, n_pages, BW, page), F32),
            pltpu.VMEM((2, n_pages, BW, page), F32),
            pltpu.SemaphoreType.DMA((2,)),
            pltpu.SemaphoreType.DMA((2,)),
            pltpu.VMEM((64, n_pages * page), F32),
        ],
    )
    return pl.pallas_call(
        functools.partial(_diff_sample_kernel, layer=i, n_pages=n_pages, page=page, lam_init=lam_init),
        grid_spec=grid_spec,
        out_shape=jax.ShapeDtypeStruct((nb, 8, BW), F32),
        compiler_params=_cp("arbitrary"),
        name="diff_sample",
    )(page_table, q8, kt, vt, bias, dl, sg4, cache_kT, cache_vT)


def _merge_kernel(x_ref, oa_ref, ob_ref, oc_ref, od_ref, g_ref, wg_ref, wb_ref, wo_ref, o_ref):
    x = x_ref[...]
    d = x.shape[1]
    un = _rms(x, g_ref[0:1, :]).astype(BF16)
    mix = jnp.zeros_like(x)
    for n, br in enumerate((oa_ref, ob_ref, oc_ref, od_ref)):
        gate = _dot(un, wg_ref[:, n * d:(n + 1) * d])
        mix = mix + _sigmoid(gate) * _dot(br[...].astype(BF16), wb_ref[n])
    o_ref[...] = x + _rms(_dot(mix.astype(BF16), wo_ref[...]), g_ref[1:2, :])


def _merge(h, oa, ob, oc, od, g2, wgates, wbr, wout, i, tm):
    n, d = h.shape
    row = pl.BlockSpec((tm, BW), lambda t: (t, 0))
    return pl.pallas_call(
        _merge_kernel,
        grid=(n // tm,),
        in_specs=[
            pl.BlockSpec((tm, d), lambda t: (t, 0)), row, row, row, row,
            pl.BlockSpec((2, d), lambda t: (0, 0)),
            pl.BlockSpec((None, d, N_BRANCH * d), lambda t: (i, 0, 0)),
            pl.BlockSpec((None, N_BRANCH, BW, d), lambda t: (i, 0, 0, 0)),
            pl.BlockSpec((None, d, d), lambda t: (i, 0, 0)),
        ],
        out_specs=pl.BlockSpec((tm, d), lambda t: (t, 0)),
        out_shape=jax.ShapeDtypeStruct((n, d), F32),
        compiler_params=_cp("parallel"),
        name="merge",
    )(h, oa, ob, oc, od, g2, wgates, wbr, wout)


def _ple_kernel(x_ref, p_ref, g_ref, wg_ref, wi_ref, o_ref):
    x = x_ref[...]
    gate = _sigmoid(_dot(_rms(x, g_ref[0:1, :]).astype(BF16), wg_ref[...]))
    e = gate * _dot(p_ref[...].astype(BF16), wi_ref[...])
    o_ref[...] = x + _rms(e, g_ref[1:2, :])


def _ple(h, p, g2, wpg, wpi, i, tm):
    n, d = h.shape
    pd = p.shape[-1]
    return pl.pallas_call(
        _ple_kernel,
        grid=(n // tm,),
        in_specs=[
            pl.BlockSpec((tm, d), lambda t: (t, 0)),
            pl.BlockSpec((None, tm, pd), lambda t: (i, t, 0)),
            pl.BlockSpec((2, d), lambda t: (0, 0)),
            pl.BlockSpec((None, d, d), lambda t: (i, 0, 0)),
            pl.BlockSpec((None, pd, d), lambda t: (i, 0, 0)),
        ],
        out_specs=pl.BlockSpec((tm, d), lambda t: (t, 0)),
        out_shape=jax.ShapeDtypeStruct((n, d), F32),
        compiler_params=_cp("parallel"),
        name="ple",
    )(h, p, g2, wpg, wpi)


def _rope_table(pos):
    def cs(d, reps):
        half = d // 2
        freqs = ROPE_THETA ** (-jnp.arange(half, dtype=F32) / half)
        ang = pos.astype(F32)[:, None] * freqs[None, :]
        c, s = jnp.cos(ang), jnp.sin(ang)
        return jnp.tile(jnp.concatenate([c, c], -1), (1, reps)), jnp.tile(jnp.concatenate([-s, s], -1), (1, reps))
    cc, sc = cs(DK_C, H_C)
    ca, sa = cs(ROPE_A, 4)
    return jnp.concatenate([cc, sc, ca, sa], axis=-1)


def _t5_bucket(n):
    max_exact = T5_BUCKETS // 2
    nf = jnp.maximum(n, 1).astype(F32)
    large = max_exact + (jnp.log(nf / max_exact) / math.log(T5_MAX_DIST / max_exact)
                         * (T5_BUCKETS - max_exact)).astype(jnp.int32)
    large = jnp.minimum(large, T5_BUCKETS - 1)
    return jnp.where(n < max_exact, n, large)


def kernel(x_prompt, x_sample, cache_mla_ckv, cache_mla_kpe, cache_diff_k, cache_diff_v, state_lru_h, state_lru_conv, state_ret, page_table, p_prompt, p_sample, t5_bias, norm_g, w_ffn_gu, w_ffn_down, w_in, w_uq, g_q, g_kv, w_uk, w_uv, conv_w, conv_b, lru_wa, lru_ba, lru_wx, lru_bx, lru_lambda, ret_gn_g, diff_lambda, diff_subln_g, w_branch, w_out, w_ple_in, w_ple_gate):
    bp, t, d = x_prompt.shape
    bs, ts, _ = x_sample.shape
    depth = w_in.shape[0]
    n_pages = page_table.shape[1]
    page = cache_mla_ckv.shape[2]
    past_len = n_pages * page
    np_, ns_ = bp * t, bs * ts
    n = np_ + ns_
    ff = w_ffn_down.shape[2]

    tm = min(512, ns_)
    assert t % tm == 0 and ns_ % tm == 0
    fc = ff // 2 if (ff // 2) % LANES == 0 else ff
    tq = min(256, t)
    rc = min(256, t)
    assert ts <= 8 and t % tq == 0 and t % RET_CHUNK == 0 and t % rc == 0

    wgu_b = w_ffn_gu.astype(BF16)
    wdn_b = w_ffn_down.astype(BF16)
    w1 = jnp.take(w_in, jnp.asarray(_w1_columns()), axis=2).astype(BF16)
    wgates = w_in[:, :, sum((256, 128, 32, 256, 256, 256, 256, 256, 256, 256, 256, 256)):].astype(BF16)
    hq = NOPE_A + ROPE_A
    nope_cols = np.concatenate([h * hq + np.arange(NOPE_A) for h in range(H_A)])
    pe_cols = np.concatenate([h * hq + NOPE_A + np.arange(ROPE_A) for h in range(H_A)])
    pes_cols = np.concatenate([h * hq + NOPE_A + _swap_halves(ROPE_A, ROPE_A) for h in range(H_A)])
    wuq = jnp.take(w_uq, jnp.asarray(np.concatenate([nope_cols, pe_cols, pes_cols])), axis=2).astype(BF16)
    eye = jnp.eye(H_A, dtype=F32)
    wuk_t = jnp.transpose(w_uk, (0, 2, 3, 1))
    wuk_bd = (eye[None, :, None, :, None] * wuk_t[:, :, :, None, :]).reshape(depth, H_A * NOPE_A, H_A * KV_LORA).astype(BF16)
    wuv_t = jnp.transpose(w_uv, (0, 2, 1, 3))
    wuv_pad = (wuv_t[:, :, :, None, :] * eye[None, :, None, :, None]).reshape(depth, H_A, KV_LORA, BW).astype(BF16)
    wuv_all = w_uv.reshape(depth, KV_LORA, H_A * V_A).astype(BF16)
    eye_l = jnp.eye(LRU_BLOCKS, dtype=F32)
    bdiag = lambda w: (eye_l[None, :, None, :, None] * w[:, :, :, None, :]).reshape(depth, LRU_W, LRU_W).astype(BF16)
    wa_bd, wx_bd = bdiag(lru_wa), bdiag(lru_wx)
    lru_vec = jnp.stack([conv_b, lru_ba, lru_bx, lru_lambda], axis=1)
    wbr_b = w_branch.astype(BF16)
    wout_b = w_out.astype(BF16)
    wpg_b = w_ple_gate.astype(BF16)
    wpi_b = w_ple_in.astype(BF16)
    subln4 = jnp.tile(diff_subln_g, (1, H_D))
    gn_b = jnp.broadcast_to(ret_gn_g[:, :, None], (depth, BW, bs))

    pos_p = jnp.arange(t, dtype=jnp.int32)
    pos_s = past_len + jnp.arange(ts, dtype=jnp.int32)
    tab = jnp.concatenate([_rope_table(pos_p), _rope_table(jnp.tile(pos_s, tm // ts))], axis=0)
    tiles_per_seq = t // tm
    n_ptiles = np_ // tm
    tab_index = lambda i: jnp.where(i < n_ptiles, i % tiles_per_seq, tiles_per_seq)

    nq = t // tq
    r_ = jnp.arange(tq, dtype=jnp.int32)
    dist = jnp.maximum(jnp.arange(nq, dtype=jnp.int32)[:, None, None] * tq + r_[None, :, None] - r_[None, None, :], 0)
    bias_p = jnp.moveaxis(t5_bias[_t5_bucket(dist)], -1, 1).astype(F32)
    k_pos_s = jnp.arange(past_len + page, dtype=jnp.int32)
    dist_s = jnp.maximum(pos_s[:, None] - k_pos_s[None, :], 0)
    bias_s = jnp.moveaxis(t5_bias[_t5_bucket(dist_s)], -1, 0).astype(F32)
    bias_s = jnp.pad(bias_s, ((0, 0), (0, 8 - ts), (0, 0))).reshape(H_D * 8, past_len + page)

    cache_kpeT = jnp.transpose(cache_mla_kpe, (0, 1, 3, 2))
    cache_dkT = jnp.transpose(cache_diff_k, (0, 1, 3, 4, 2)).reshape(depth, -1, BW, page)
    cache_dvT = jnp.transpose(cache_diff_v, (0, 1, 3, 4, 2)).reshape(depth, -1, BW, page)
    state_retT = jnp.transpose(state_ret, (0, 2, 3, 4, 1))
    conv_sT = jnp.transpose(state_lru_conv, (0, 2, 1, 3))

    h = jnp.concatenate([x_prompt.reshape(np_, d), x_sample.reshape(ns_, d)], axis=0)
    p_all = jnp.concatenate([p_prompt.reshape(depth, np_, -1), p_sample.reshape(depth, ns_, -1)], axis=1)

    def to_tb(a):
        return jnp.transpose(a.reshape(bs, ts, -1), (1, 0, 2))

    def to_twb(a):
        return jnp.transpose(a.reshape(bs, ts, -1), (1, 2, 0)).astype(F32)

    def tail_T(a):
        return jnp.pad(jnp.transpose(a.reshape(bs, ts, -1), (0, 2, 1)), ((0, 0), (0, 0), (0, page - ts)))

    outs = [[] for _ in range(14)]
    for i in range(depth):
        lam_init = 0.8 - 0.6 * math.exp(-0.3 * i)
        ng = norm_g[i]
        h = _ffn(h, ng[0:2], wgu_b, wdn_b, i, 0, tm, fc)
        (ckvn, kpe4, kcat, qlat, qpe, lgate, lx, rq, rk, rv, rg, dq, dk, dv, dkb, dvb) = _inproj(
            h, ng[2:3], w1, tab, g_q[i:i + 1], g_kv[i:i + 1], wuq, wuk_bd, i, tm, tab_index)
        kpe = kpe4[:, :ROPE_A]

        oa_p = _mla_prompt(qlat, qpe, kcat, wuv_pad, i, bp, t, tq)
        od_p = _diff_prompt(dq, dkb, dvb, bias_p, diff_lambda, diff_subln_g[i:i + 1], i, bp, t, tq, lam_init)
        oc_p, s_full = _ret_prompt(rq, rk, rv, rg, ret_gn_g[i:i + 1], i, bp, t)
        ob_p, hl_p = _lru_prompt(lx, lgate, conv_w, lru_vec, wa_bd, wx_bd, i, bp, t, rc)

        sl = slice(np_, n)
        ql_s = jnp.pad(jnp.transpose(qlat[sl].reshape(bs, ts, H_A, KV_LORA), (0, 2, 1, 3)),
                       ((0, 0), (0, 0), (0, 8 - ts), (0, 0))).reshape(bs, H_A * 8, KV_LORA)
        qp_s = jnp.pad(jnp.transpose(qpe[sl].reshape(bs, ts, H_A, ROPE_A), (0, 2, 1, 3)),
                       ((0, 0), (0, 0), (0, 8 - ts), (0, 0))).reshape(bs, H_A * 8, ROPE_A)
        ckv_tail = jnp.pad(ckvn[sl].reshape(bs, ts, KV_LORA), ((0, 0), (0, page - ts), (0, 0)))
        oa_s = _mla_sample(page_table, ql_s, qp_s, ckv_tail, tail_T(kpe[sl]), cache_mla_ckv, cache_kpeT, wuv_all, i)
        dq8 = jnp.pad(dq[sl].reshape(bs, ts, BW), ((0, 0), (0, 8 - ts), (0, 0))).astype(F32)
        od_s = _diff_sample(page_table, dq8, tail_T(dk[sl]), tail_T(dv[sl]), bias_s, diff_lambda, subln4[i:i + 1],
                            cache_dkT, cache_dvT, i, lam_init)
        oc_sT, s_new = _ret_sample(to_twb(rq[sl]), to_twb(rk[sl]), to_twb(rv[sl]), to_twb(rg[sl]),
                                   gn_b[i], state_retT, i)
        ob_s, hl_s = _lru_sample(to_tb(lx[sl]), to_tb(lgate[sl]), conv_sT, state_lru_h, conv_w, lru_vec,
                                 wa_bd, wx_bd, i)

        oa = jnp.concatenate([oa_p, oa_s[:, :ts].reshape(ns_, BW)], axis=0)
        ob = jnp.concatenate([ob_p, jnp.transpose(ob_s, (1, 0, 2)).reshape(ns_, BW)], axis=0)
        oc = jnp.concatenate([oc_p, jnp.transpose(oc_sT, (2, 0, 1)).reshape(ns_, BW)], axis=0)
        od = jnp.concatenate([od_p, od_s[:, :ts].reshape(ns_, BW)], axis=0)
        h = _merge(h, oa, ob, oc, od, ng[2:4], wgates, wbr_b, wout_b, i, tm)
        h = _ffn(h, ng[4:6], wgu_b, wdn_b, i, 1, tm, fc)
        h = _ple(h, p_all, ng[6:8], wpg_b, wpi_b, i, tm)

        lx_p = lx[:np_].reshape(bp, t, LRU_W)
        s5 = s_full.reshape(bp, H_C, DK_C, H_C, DV_C)
        ret_p = jnp.stack([s5[:, hh, :, hh, :] for hh in range(H_C)], axis=1)
        vals = [ckvn[:np_].reshape(bp, t, KV_LORA), kpe[:np_].reshape(bp, t, ROPE_A),
                dk[:np_].reshape(bp, t, H_D, 2 * DH_D), dv[:np_].reshape(bp, t, H_D, DV_D),
                hl_p.reshape(bp, LRU_W), lx_p[:, t - (CONV_W - 1):], ret_p,
                ckvn[sl].reshape(bs, ts, KV_LORA), kpe[sl].reshape(bs, ts, ROPE_A),
                dk[sl].reshape(bs, ts, H_D, 2 * DH_D), dv[sl].reshape(bs, ts, H_D, DV_D),
                hl_s, lx[sl].reshape(bs, ts, LRU_W)[:, ts - (CONV_W - 1):],
                jnp.transpose(s_new, (3, 0, 1, 2))]
        for lst, v in zip(outs, vals):
            lst.append(v)

    stacked = [jnp.stack(a) for a in outs]
    return (h[:np_].reshape(bp, t, d), h[np_:].reshape(bs, ts, d), *stacked)
```

```python
import functools
import math

import numpy as np
import jax
import jax.numpy as jnp
from jax import lax
from jax.experimental import pallas as pl
from jax.experimental.pallas import tpu as pltpu

F32 = jnp.float32
BF16 = jnp.bfloat16

N_BRANCH = 4
H_A, Q_LORA, KV_LORA, NOPE_A, ROPE_A, V_A = 4, 256, 128, 64, 32, 64
ROPE_THETA = 10000.0
LRU_W, LRU_BLOCKS, CONV_W, LRU_C = 256, 4, 4, 8.0
H_C, DK_C, DV_C, RET_CHUNK = 4, 64, 64, 128
H_D, DH_D, DV_D = 4, 32, 64
T5_BUCKETS, T5_MAX_DIST = 32, 128
EPS = 1e-6
NEG_INF = -1e30
BW = 256

VMEM_LIMIT = 56 * 1024 * 1024
LANES = 128

_NT = (((1,), (1,)), ((), ()))
_TN = (((0,), (0,)), ((), ()))


def _cp(*sem):
    return pltpu.CompilerParams(dimension_semantics=sem, vmem_limit_bytes=VMEM_LIMIT)


def _rms(x, g):
    return x * lax.rsqrt(jnp.mean(x * x, axis=-1, keepdims=True) + EPS) * g


def _dot(a, b):
    return jnp.dot(a, b, preferred_element_type=F32)


def _dot_nt(a, b):
    return lax.dot_general(a, b, _NT, preferred_element_type=F32)


def _sigmoid(x):
    return 1.0 / (1.0 + jnp.exp(-x))


def _ffn_kernel(x_ref, g_ref, wg_ref, wu_ref, wd_ref, o_ref, xn_ref, acc_ref):
    j = pl.program_id(1)

    @pl.when(j == 0)
    def _():
        xn_ref[...] = _rms(x_ref[...], g_ref[0:1, :]).astype(BF16)
        acc_ref[...] = jnp.zeros_like(acc_ref)

    xn = xn_ref[...]
    g = _dot(xn, wg_ref[...])
    u = _dot(xn, wu_ref[...])
    a = (g * _sigmoid(g) * u).astype(BF16)
    acc_ref[...] += _dot(a, wd_ref[...])

    @pl.when(j == pl.num_programs(1) - 1)
    def _():
        o_ref[...] = x_ref[...] + 0.5 * _rms(acc_ref[...], g_ref[1:2, :])


def _ffn(h, g2, wgu, wdn, i, k, tm, fc):
    n, d = h.shape
    ff = wdn.shape[2]
    nf = ff // fc
    return pl.pallas_call(
        _ffn_kernel,
        grid=(n // tm, nf),
        in_specs=[
            pl.BlockSpec((tm, d), lambda t, j: (t, 0)),
            pl.BlockSpec((2, d), lambda t, j: (0, 0)),
            pl.BlockSpec((None, None, d, fc), lambda t, j: (i, k, 0, j)),
            pl.BlockSpec((None, None, d, fc), lambda t, j: (i, k, 0, nf + j)),
            pl.BlockSpec((None, None, fc, d), lambda t, j: (i, k, j, 0)),
        ],
        out_specs=pl.BlockSpec((tm, d), lambda t, j: (t, 0)),
        out_shape=jax.ShapeDtypeStruct((n, d), F32),
        scratch_shapes=[pltpu.VMEM((tm, d), BF16), pltpu.VMEM((tm, d), F32)],
        compiler_params=_cp("parallel", "arbitrary"),
        name="ffn",
    )(h, g2, wgu, wgu, wdn)


_C_CQ, _C_CKV, _C_KPE, _C_KPES, _C_LG, _C_LX = 0, 256, 384, 512, 640, 896
_C_RQ, _C_RK, _C_RV, _C_RG, _C_DQ, _C_DK, _C_DV, _C_RQS, _C_RKS = 1152, 1408, 1664, 1920, 2176, 2432, 2688, 2944, 3200
_W1_COLS = 3456


def _swap_halves(width, group):
    idx = np.arange(width)
    half = group // 2
    return np.where(idx % group < half, idx + half, idx - half)


def _w1_columns():
    o_cq, o_ckv, o_kpe, o_lg, o_lx = 0, 256, 384, 416, 672
    o_rq, o_rk, o_rv, o_rg, o_dq, o_dk, o_dv = 928, 1184, 1440, 1696, 1952, 2208, 2464
    r = np.arange
    kpe = o_kpe + np.tile(r(ROPE_A), 4)
    kpe_s = o_kpe + np.tile(_swap_halves(ROPE_A, ROPE_A), 4)
    cols = np.concatenate([
        o_cq + r(256), o_ckv + r(128), kpe, kpe_s, o_lg + r(256), o_lx + r(256),
        o_rq + r(256), o_rk + r(256), o_rv + r(256), o_rg + r(256),
        o_dq + r(256), o_dk + r(256), o_dv + r(256),
        o_rq + _swap_halves(256, DK_C), o_rk + _swap_halves(256, DK_C)])
    assert cols.shape[0] == _W1_COLS
    return cols


def _take_cols(w, cols):
    cols = np.asarray(cols)
    cuts = np.flatnonzero(np.diff(cols) != 1) + 1
    runs = np.split(cols, cuts)
    return jnp.concatenate([w[..., int(r[0]):int(r[-1]) + 1] for r in runs], axis=-1)


def _inproj_kernel(x_ref, g_ref, w1_ref, tab_ref, gq_ref, gkv_ref, wuq_ref, wuk_ref,
                   ckv_o, kpe_o, kcat_o, qlat_o, qpe_o, lg_o, lx_o, rq_o, rk_o, rv_o, rg_o,
                   dq_o, dk_o, dv_o, dkb_o, dvb_o):
    un = _rms(x_ref[...], g_ref[...]).astype(BF16)
    z = _dot(un, w1_ref[...])
    cosc, sinc = tab_ref[:, 0:256], tab_ref[:, 256:512]
    cosa, sina = tab_ref[:, 512:640], tab_ref[:, 640:768]

    cqn = _rms(z[:, _C_CQ:_C_CQ + 256], gq_ref[...]).astype(BF16)
    q = _dot(cqn, wuq_ref[...])
    qpe_o[...] = (q[:, 256:384] * cosa + q[:, 384:512] * sina).astype(BF16)
    qlat_o[...] = _dot(q[:, 0:256].astype(BF16), wuk_ref[...]).astype(BF16)

    ckvn = _rms(z[:, _C_CKV:_C_CKV + 128], gkv_ref[...])
    kpe4 = z[:, _C_KPE:_C_KPE + 128] * cosa + z[:, _C_KPES:_C_KPES + 128] * sina
    ckv_o[...] = ckvn
    kpe_o[...] = kpe4
    kcat_o[:, 0:128] = ckvn.astype(BF16)
    kcat_o[:, 128:256] = kpe4.astype(BF16)

    lg_o[...] = z[:, _C_LG:_C_LG + 256]
    lx_o[...] = z[:, _C_LX:_C_LX + 256]
    rq_o[...] = (z[:, _C_RQ:_C_RQ + 256] * cosc + z[:, _C_RQS:_C_RQS + 256] * sinc).astype(BF16)
    rk_o[...] = (z[:, _C_RK:_C_RK + 256] * cosc + z[:, _C_RKS:_C_RKS + 256] * sinc) * (DK_C ** -0.5)
    rv_o[...] = z[:, _C_RV:_C_RV + 256].astype(BF16)
    rg_o[...] = z[:, _C_RG:_C_RG + 256]
    dq_o[...] = z[:, _C_DQ:_C_DQ + 256].astype(BF16)
    dk = z[:, _C_DK:_C_DK + 256]
    dv = z[:, _C_DV:_C_DV + 256]
    dk_o[...] = dk
    dv_o[...] = dv
    dkb_o[...] = dk.astype(BF16)
    dvb_o[...] = dv.astype(BF16)


def _inproj(h, g, w1, tab, gq, gkv, wuq, wuk, i, tm, tab_index):
    n, d = h.shape
    widths = [(128, F32), (128, F32), (256, BF16), (512, BF16), (128, BF16), (256, F32), (256, F32),
              (256, BF16), (256, F32), (256, BF16), (256, F32), (256, BF16), (256, F32), (256, F32),
              (256, BF16), (256, BF16)]
    return pl.pallas_call(
        _inproj_kernel,
        grid=(n // tm,),
        in_specs=[
            pl.BlockSpec((tm, d), lambda t: (t, 0)),
            pl.BlockSpec((1, d), lambda t: (0, 0)),
            pl.BlockSpec((None, d, _W1_COLS), lambda t: (i, 0, 0)),
            pl.BlockSpec((tm, 768), lambda t: (tab_index(t), 0)),
            pl.BlockSpec((1, Q_LORA), lambda t: (0, 0)),
            pl.BlockSpec((1, KV_LORA), lambda t: (0, 0)),
            pl.BlockSpec((None, Q_LORA, 512), lambda t: (i, 0, 0)),
            pl.BlockSpec((None, 256, 512), lambda t: (i, 0, 0)),
        ],
        out_specs=[pl.BlockSpec((tm, w), lambda t: (t, 0)) for w, _ in widths],
        out_shape=[jax.ShapeDtypeStruct((n, w), dt) for w, dt in widths],
        compiler_params=_cp("parallel"),
        name="inproj",
    )(h, g, w1, tab, gq, gkv, wuq, wuk)


def _mla_prompt_kernel(qlat_ref, qpe_ref, k_ref, wuv_ref, o_ref, q4_ref, *, tq):
    qi = pl.program_id(1)
    scale = (NOPE_A + ROPE_A) ** -0.5
    qpe = qpe_ref[...].astype(F32)
    lane = lax.broadcasted_iota(jnp.int32, (tq, LANES), 1)
    for h in range(H_A):
        q4_ref[h * tq:(h + 1) * tq, 0:KV_LORA] = qlat_ref[:, h * KV_LORA:(h + 1) * KV_LORA]
        q4_ref[h * tq:(h + 1) * tq, KV_LORA:2 * KV_LORA] = jnp.where(lane // ROPE_A == h, qpe, 0.0).astype(BF16)
    row = lax.broadcasted_iota(jnp.int32, (tq, tq), 0)
    col = lax.broadcasted_iota(jnp.int32, (tq, tq), 1)

    def keys(j):
        return k_ref[pl.ds(pl.multiple_of(j * tq, tq), tq), :]

    def scores(j):
        return _dot_nt(q4_ref[...], keys(j))

    def update(j, s_all, state, masked):
        ms, ls, accs = state
        v = keys(j)[:, 0:KV_LORA]
        ms2, ls2, accs2 = [], [], []
        for h in range(H_A):
            s = s_all[h * tq:(h + 1) * tq, :] * scale
            if masked:
                s = jnp.where(col <= row, s, NEG_INF)
            m_new = jnp.maximum(ms[h], jnp.max(s, axis=-1, keepdims=True))
            alpha = jnp.exp(ms[h] - m_new)
            p = jnp.exp(s - m_new)
            ls2.append(alpha * ls[h] + jnp.sum(p, axis=-1, keepdims=True))
            accs2.append(alpha * accs[h] + _dot(p.astype(BF16), v))
            ms2.append(m_new)
        return tuple(ms2), tuple(ls2), tuple(accs2)

    def step(j, state, masked):
        return update(j, scores(j), state, masked)

    init = (tuple(jnp.full((tq, 1), NEG_INF, F32) for _ in range(H_A)),
            tuple(jnp.zeros((tq, 1), F32) for _ in range(H_A)),
            tuple(jnp.zeros((tq, KV_LORA), F32) for _ in range(H_A)))
    state = lax.fori_loop(0, qi, functools.partial(step, masked=False), init)
    _, ls, accs = step(qi, state, True)
    out = jnp.zeros((tq, BW), F32)
    for h in range(H_A):
        out = out + _dot((accs[h] / ls[h]).astype(BF16), wuv_ref[h])
    o_ref[...] = out


def _mla_prompt(qlat, qpe, kcat, wuvp, i, b, t, tq):
    nq = t // tq
    return pl.pallas_call(
        functools.partial(_mla_prompt_kernel, tq=tq),
        grid=(b, nq),
        in_specs=[
            pl.BlockSpec((tq, 512), lambda bb, qi: (bb * nq + qi, 0)),
            pl.BlockSpec((tq, 128), lambda bb, qi: (bb * nq + qi, 0)),
            pl.BlockSpec((t, 256), lambda bb, qi: (bb, 0)),
            pl.BlockSpec((None, H_A, KV_LORA, BW), lambda bb, qi: (i, 0, 0, 0)),
        ],
        out_specs=pl.BlockSpec((tq, BW), lambda bb, qi: (bb * nq + qi, 0)),
        out_shape=jax.ShapeDtypeStruct((b * t, BW), F32),
        scratch_shapes=[pltpu.VMEM((H_A * tq, 2 * KV_LORA), BF16)],
        compiler_params=_cp("parallel", "arbitrary"),
        name="mla_prompt",
    )(qlat, qpe, kcat, wuvp)


def _diff_lambda(dl, lam_init):
    a = jnp.sum(dl[0:1, :] * dl[1:2, :], axis=-1, keepdims=True)
    b = jnp.sum(dl[2:3, :] * dl[3:4, :], axis=-1, keepdims=True)
    return jnp.exp(a) - jnp.exp(b) + lam_init


def _diff_prompt_kernel(dq_ref, k_ref, v_ref, bias_ref, dl_ref, sg_ref, o_ref, q8_ref, *, tq, lam_init):
    qi = pl.program_id(1)
    scale = DH_D ** -0.5
    lam = _diff_lambda(dl_ref[...], lam_init)
    q = dq_ref[...].astype(F32)
    lane = lax.broadcasted_iota(jnp.int32, (tq, BW), 1)
    for hc in range(2 * H_D):
        q8_ref[hc * tq:(hc + 1) * tq, :] = jnp.where(lane // DH_D == hc, q, 0.0).astype(BF16)
    row = lax.broadcasted_iota(jnp.int32, (tq, tq), 0)
    col = lax.broadcasted_iota(jnp.int32, (tq, tq), 1)

    def scores(j):
        return _dot_nt(q8_ref[...], k_ref[pl.ds(pl.multiple_of(j * tq, tq), tq), :])

    def update(j, s_all, state, masked):
        ms, ls, accs = state
        v = v_ref[pl.ds(pl.multiple_of(j * tq, tq), tq), :]
        ms2, ls2, accs2 = [], [], []
        for hc in range(2 * H_D):
            h = hc // 2
            s = s_all[hc * tq:(hc + 1) * tq, :] * scale + bias_ref[qi - j, h]
            if masked:
                s = jnp.where(col <= row, s, NEG_INF)
            m_new = jnp.maximum(ms[hc], jnp.max(s, axis=-1, keepdims=True))
            alpha = jnp.exp(ms[hc] - m_new)
            p = jnp.exp(s - m_new)
            ls2.append(alpha * ls[hc] + jnp.sum(p, axis=-1, keepdims=True))
            vh = v[:, (h // 2) * LANES:(h // 2 + 1) * LANES]
            accs2.append(alpha * accs[hc] + _dot(p.astype(BF16), vh))
            ms2.append(m_new)
        return tuple(ms2), tuple(ls2), tuple(accs2)

    n = 2 * H_D
    init = (tuple(jnp.full((tq, 1), NEG_INF, F32) for _ in range(n)),
            tuple(jnp.zeros((tq, 1), F32) for _ in range(n)),
            tuple(jnp.zeros((tq, LANES), F32) for _ in range(n)))

    def step(j, state, masked):
        return update(j, scores(j), state, masked)

    state = lax.fori_loop(0, qi, functools.partial(step, masked=False), init)
    _, ls, accs = step(qi, state, True)
    sg = sg_ref[...]
    for h in range(H_D):
        o = accs[2 * h] / ls[2 * h] - lam * (accs[2 * h + 1] / ls[2 * h + 1])
        oh = o[:, (h % 2) * DV_D:(h % 2 + 1) * DV_D]
        o_ref[:, h * DV_D:(h + 1) * DV_D] = _rms(oh, sg) * (1.0 - lam_init)


def _diff_prompt(dq, dkb, dvb, bias, dl, sg, i, b, t, tq, lam_init):
    nq = t // tq
    return pl.pallas_call(
        functools.partial(_diff_prompt_kernel, tq=tq, lam_init=lam_init),
        grid=(b, nq),
        in_specs=[
            pl.BlockSpec((tq, BW), lambda bb, qi: (bb * nq + qi, 0)),
            pl.BlockSpec((t, BW), lambda bb, qi: (bb, 0)),
            pl.BlockSpec((t, BW), lambda bb, qi: (bb, 0)),
            pl.BlockSpec((nq, H_D, tq, tq), lambda bb, qi: (0, 0, 0, 0)),
            pl.BlockSpec((None, 4, DH_D), lambda bb, qi: (i, 0, 0)),
            pl.BlockSpec((1, DV_D), lambda bb, qi: (0, 0)),
        ],
        out_specs=pl.BlockSpec((tq, BW), lambda bb, qi: (bb * nq + qi, 0)),
        out_shape=jax.ShapeDtypeStruct((b * t, BW), F32),
        scratch_shapes=[pltpu.VMEM((2 * H_D * tq, BW), BF16)],
        compiler_params=_cp("parallel", "arbitrary"),
        name="diff_prompt",
    )(dq, dkb, dvb, bias, dl, sg)


def _ret_log_gamma(h):
    return math.log1p(-(2.0 ** (-5.0 - h)))


def _lane_heads(shape, axis, width):
    return lax.broadcasted_iota(jnp.int32, shape, axis) // width


def _per_head(idx, vals):
    out = jnp.full(idx.shape, vals[-1], F32)
    for h in range(len(vals) - 2, -1, -1):
        out = jnp.where(idx == h, vals[h], out)
    return out


def _group_norm_lanes(o, headl, n_heads, width):
    mu = jnp.zeros_like(o)
    for h in range(n_heads):
        mh = headl == h
        s = jnp.sum(jnp.where(mh, o, 0.0), axis=-1, keepdims=True) * (1.0 / width)
        mu = jnp.where(mh, s, mu)
    d = o - mu
    var = jnp.zeros_like(o)
    for h in range(n_heads):
        mh = headl == h
        s = jnp.sum(jnp.where(mh, d * d, 0.0), axis=-1, keepdims=True) * (1.0 / width)
        var = jnp.where(mh, s, var)
    return d * lax.rsqrt(var + EPS)


def _ret_prompt_kernel(q_ref, k_ref, v_ref, g_ref, gn_ref, o_ref, s_ref, s_scr, *, n_chunks):
    c = RET_CHUNK
    lgs = [_ret_log_gamma(h) for h in range(H_C)]
    headl = _lane_heads((c, BW), 1, DK_C)
    lgl = _per_head(headl, lgs)
    idx = lax.broadcasted_iota(jnp.int32, (c, BW), 0).astype(F32)
    qdec = jnp.exp((idx + 1.0) * lgl)
    kdec = jnp.exp((c - 1.0 - idx) * lgl)
    cdec = jnp.exp(float(c) * _per_head(_lane_heads((BW, BW), 0, DK_C), lgs))
    ri = lax.broadcasted_iota(jnp.int32, (c, c), 0)
    ci = lax.broadcasted_iota(jnp.int32, (c, c), 1)
    dif = (ri - ci).astype(F32)
    intra = [jnp.where(dif >= 0, jnp.exp(jnp.maximum(dif, 0.0) * lgs[h]), 0.0) for h in range(H_C)]
    bd = _lane_heads((BW, BW), 0, DK_C) == _lane_heads((BW, BW), 1, DV_C)
    s_scr[...] = jnp.zeros_like(s_scr)

    def chunk(ic, _):
        rows = pl.ds(pl.multiple_of(ic * c, c), c)
        q = q_ref[rows, :]
        k = k_ref[rows, :]
        v = v_ref[rows, :]
        s = s_scr[...]
        o = _dot(q, s.astype(BF16)) * qdec
        for h in range(H_C):
            mh = headl == h
            att = _dot_nt(q, jnp.where(mh, k, 0.0).astype(BF16)) * intra[h]
            o = o + jnp.where(mh, _dot(att.astype(BF16), v), 0.0)
        kd = (k * kdec).astype(BF16)
        upd = lax.dot_general(kd, v, _TN, preferred_element_type=F32)
        s_scr[...] = cdec * s + jnp.where(bd, upd, 0.0)
        y = _group_norm_lanes(o, headl, H_C, DV_C) * gn_ref[...]
        g = g_ref[rows, :]
        o_ref[rows, :] = g * _sigmoid(g) * y
        return 0

    lax.fori_loop(0, n_chunks, chunk, 0)
    s_ref[...] = s_scr[...]


def _ret_prompt(rq, rk, rv, rg, gn, i, b, t):
    return pl.pallas_call(
        functools.partial(_ret_prompt_kernel, n_chunks=t // RET_CHUNK),
        grid=(b,),
        in_specs=[pl.BlockSpec((t, BW), lambda bb: (bb, 0))] * 4 + [pl.BlockSpec((1, BW), lambda bb: (0, 0))],
        out_specs=[pl.BlockSpec((t, BW), lambda bb: (bb, 0)), pl.BlockSpec((None, BW, BW), lambda bb: (bb, 0, 0))],
        out_shape=[jax.ShapeDtypeStruct((b * t, BW), F32), jax.ShapeDtypeStruct((b, BW, BW), F32)],
        scratch_shapes=[pltpu.VMEM((BW, BW), F32)],
        compiler_params=_cp("parallel"),
        name="ret_prompt",
    )(rq, rk, rv, rg, gn)


def _lru_gates(conv, wa, wx, vec):
    cb = conv.astype(BF16)
    r = _sigmoid(_dot(cb, wa) + vec[1:2, :])
    ig = _sigmoid(_dot(cb, wx) + vec[2:3, :])
    nl = -vec[3:4, :]
    sp = jnp.maximum(nl, 0.0) + jnp.log1p(jnp.exp(-jnp.abs(nl)))
    log_a = -LRU_C * r * sp
    a = jnp.exp(log_a)
    u = jnp.sqrt(-jnp.tanh(log_a) * (a * a + 1.0)) * (ig * conv)
    return a, u


def _gelu(x):
    return 0.5 * x * (1.0 + jnp.tanh(math.sqrt(2.0 / math.pi) * (x + 0.044715 * (x * x * x))))


def _lru_prompt_kernel(x_ref, g_ref, cw_ref, vec_ref, wa_ref, wx_ref, y_ref, hl_ref,
                       xp, a_s, u_s, h_s, *, rc):
    ic = pl.program_id(1)
    pad = 8

    @pl.when(ic == 0)
    def _():
        xp[0:pad, :] = jnp.zeros((pad, LRU_W), F32)
        h_s[...] = jnp.zeros_like(h_s)

    xp[pad:pad + rc, :] = x_ref[...]
    vec = vec_ref[...]
    conv = vec[0:1, :] + cw_ref[3:4, :] * xp[pad:pad + rc, :]
    for j in range(CONV_W - 1):
        k = CONV_W - 1 - j
        conv = conv + cw_ref[j:j + 1, :] * xp[pad - k:pad - k + rc, :]
    xp[0:pad, :] = xp[rc:rc + pad, :]

    a, u = _lru_gates(conv, wa_ref[...], wx_ref[...], vec)
    a_s[0:pad, :] = jnp.ones((pad, LRU_W), F32)
    u_s[0:pad, :] = jnp.zeros((pad, LRU_W), F32)
    a_s[pad:pad + rc, :] = a
    u_s[pad:pad + rc, :] = u
    tmod = lax.broadcasted_iota(jnp.int32, (rc, LRU_W), 0) % pad
    for d in (1, 2, 4):
        a_cur, u_cur = a_s[pad:pad + rc, :], u_s[pad:pad + rc, :]
        a_sh, u_sh = a_s[pad - d:pad - d + rc, :], u_s[pad - d:pad - d + rc, :]
        m = tmod >= d
        a_s[pad:pad + rc, :] = jnp.where(m, a_cur * a_sh, a_cur)
        u_s[pad:pad + rc, :] = jnp.where(m, a_cur * u_sh + u_cur, u_cur)
    h = h_s[...]
    for blk in range(rc // pad):
        lo = pad + blk * pad
        hs = a_s[lo:lo + pad, :] * h + u_s[lo:lo + pad, :]
        u_s[lo:lo + pad, :] = hs
        h = hs[pad - 1:pad, :]
    h_s[...] = h
    hl_ref[...] = h
    y_ref[...] = u_s[pad:pad + rc, :] * _gelu(g_ref[...])


def _lru_prompt(lx, lgate, cw, vec, wa, wx, i, b, t, rc):
    nc = t // rc
    return pl.pallas_call(
        functools.partial(_lru_prompt_kernel, rc=rc),
        grid=(b, nc),
        in_specs=[
            pl.BlockSpec((rc, LRU_W), lambda bb, c: (bb * nc + c, 0)),
            pl.BlockSpec((rc, LRU_W), lambda bb, c: (bb * nc + c, 0)),
            pl.BlockSpec((None, CONV_W, LRU_W), lambda bb, c: (i, 0, 0)),
            pl.BlockSpec((None, 4, LRU_W), lambda bb, c: (i, 0, 0)),
            pl.BlockSpec((None, LRU_W, LRU_W), lambda bb, c: (i, 0, 0)),
            pl.BlockSpec((None, LRU_W, LRU_W), lambda bb, c: (i, 0, 0)),
        ],
        out_specs=[pl.BlockSpec((rc, LRU_W), lambda bb, c: (bb * nc + c, 0)),
                   pl.BlockSpec((None, 1, LRU_W), lambda bb, c: (bb, 0, 0))],
        out_shape=[jax.ShapeDtypeStruct((b * t, LRU_W), F32), jax.ShapeDtypeStruct((b, 1, LRU_W), F32)],
        scratch_shapes=[pltpu.VMEM((rc + 8, LRU_W), F32), pltpu.VMEM((rc + 8, LRU_W), F32),
                        pltpu.VMEM((rc + 8, LRU_W), F32), pltpu.VMEM((1, LRU_W), F32)],
        compiler_params=_cp("parallel", "arbitrary"),
        name="lru_prompt",
    )(lx, lgate, cw, vec, wa, wx)


def _lru_sample_kernel(x_ref, g_ref, cb_ref, h0_ref, cw_ref, vec_ref, wa_ref, wx_ref, y_ref, hl_ref, *, t):
    vec = vec_ref[...]
    xs = [cb_ref[j] for j in range(CONV_W - 1)] + [x_ref[j] for j in range(t)]
    h = h0_ref[...]
    for n in range(t):
        conv = vec[0:1, :]
        for j in range(CONV_W):
            conv = conv + cw_ref[j:j + 1, :] * xs[n + j]
        a, u = _lru_gates(conv, wa_ref[...], wx_ref[...], vec)
        h = a * h + u
        y_ref[n] = h * _gelu(g_ref[n])
    hl_ref[...] = h


def _lru_sample(xs, gs, cb, h0, cw, vec, wa, wx, i):
    t, b, w = xs.shape
    full = lambda *shape: pl.BlockSpec(shape, lambda g: (0,) * len(shape))
    lay = lambda *shape: pl.BlockSpec((None,) + shape, lambda g: (i,) + (0,) * len(shape))
    return pl.pallas_call(
        functools.partial(_lru_sample_kernel, t=t),
        grid=(1,),
        in_specs=[full(t, b, w), full(t, b, w), lay(CONV_W - 1, b, w), lay(b, w),
                  lay(CONV_W, w), lay(4, w), lay(w, w), lay(w, w)],
        out_specs=[full(t, b, w), full(b, w)],
        out_shape=[jax.ShapeDtypeStruct((t, b, w), F32), jax.ShapeDtypeStruct((b, w), F32)],
        compiler_params=_cp("arbitrary"),
        name="lru_sample",
    )(xs, gs, cb, h0, cw, vec, wa, wx)


def _ret_sample_kernel(q_ref, k_ref, v_ref, g_ref, gn_ref, s_ref, o_ref, sn_ref, *, t):
    for h in range(H_C):
        lg = _ret_log_gamma(h)
        lo = h * DK_C
        qs = [q_ref[n, lo:lo + DK_C, :] for n in range(t)]
        ks = [k_ref[n, lo:lo + DK_C, :] for n in range(t)]
        vs = [v_ref[n, lo:lo + DV_C, :] for n in range(t)]
        os_ = []
        for n in range(t):
            o = jnp.zeros_like(vs[0])
            for m in range(n + 1):
                att = jnp.sum(qs[n] * ks[m], axis=0, keepdims=True) * math.exp((n - m) * lg)
                o = o + att * vs[m]
            os_.append(o)
        kdec = [math.exp((t - 1 - m) * lg) for m in range(t)]
        qdec = [math.exp((n + 1) * lg) for n in range(t)]
        cdec = math.exp(t * lg)

        def body(d, carry, h=h, lo=lo, vs=vs, kdec=kdec, cdec=cdec):
            srow = s_ref[h, d]
            new = cdec * srow
            out = []
            for n in range(t):
                qd = q_ref[n, pl.ds(lo + d, 1), :]
                kd = k_ref[n, pl.ds(lo + d, 1), :]
                out.append(carry[n] + qd * srow)
                new = new + (kd * kdec[n]) * vs[n]
            sn_ref[h, d] = new
            return tuple(out)

        cross = lax.fori_loop(0, DK_C, body, tuple(jnp.zeros_like(vs[0]) for _ in range(t)))
        gn = gn_ref[lo:lo + DV_C, :]
        for n in range(t):
            o = os_[n] + cross[n] * qdec[n]
            mu = jnp.mean(o, axis=0, keepdims=True)
            d = o - mu
            var = jnp.mean(d * d, axis=0, keepdims=True)
            y = d * lax.rsqrt(var + EPS) * gn
            g = g_ref[n, lo:lo + DV_C, :]
            o_ref[n, lo:lo + DV_C, :] = g * _sigmoid(g) * y


def _ret_sample(qT, kT, vT, gT, gnb, state, i):
    t, w, b = qT.shape
    full = lambda *shape: pl.BlockSpec(shape, lambda g: (0,) * len(shape))
    return pl.pallas_call(
        functools.partial(_ret_sample_kernel, t=t),
        grid=(1,),
        in_specs=[full(t, w, b)] * 4 + [full(w, b),
                  pl.BlockSpec((None, H_C, DK_C, DV_C, b), lambda g: (i, 0, 0, 0, 0))],
        out_specs=[full(t, w, b), full(H_C, DK_C, DV_C, b)],
        out_shape=[jax.ShapeDtypeStruct((t, w, b), F32), jax.ShapeDtypeStruct((H_C, DK_C, DV_C, b), F32)],
        compiler_params=_cp("arbitrary"),
        name="ret_sample",
    )(qT, kT, vT, gT, gnb, state)


_DMA_UNROLL = 8
_PAGE_CHUNK = 8


def _page_copies(pt_ref, hbm, buf, sem, layer, b, slot):
    def at(j):
        return pltpu.make_async_copy(hbm.at[layer, pt_ref[b, j]], buf.at[slot, j], sem.at[slot])
    return at


def _start_pages(at, n_pages):
    def body(j, _):
        at(j).start()
        return 0
    lax.fori_loop(0, n_pages, body, 0, unroll=_DMA_UNROLL)


def _wait_pages(at, n_pages):
    def body(j, _):
        at(j).wait()
        return 0
    lax.fori_loop(0, n_pages, body, 0, unroll=_DMA_UNROLL)


def _pipeline_pages(copies, n_pages):
    b = pl.program_id(0)
    slot = b % 2

    @pl.when(b == 0)
    def _():
        for at in copies(0, 0):
            _start_pages(at, n_pages)

    @pl.when(b + 1 < pl.num_programs(0))
    def _():
        for at in copies(b + 1, 1 - slot):
            _start_pages(at, n_pages)

    for at in copies(b, slot):
        _wait_pages(at, n_pages)
    return slot


def _row_max(x):
    return jnp.max(x, axis=-1, keepdims=True)


def _row_sum(x):
    return jnp.sum(x, axis=-1, keepdims=True)


def _mla_sample_kernel(pt_ref, qc_ref, qp_ref, ckvn_ref, kpen_ref, ckv_hbm, kpe_hbm, wuv_ref, o_ref,
                       ckv_buf, kpe_buf, sem_c, sem_k, knew, *, layer, n_pages, page, ts):
    scale = (NOPE_A + ROPE_A) ** -0.5

    @pl.when(pl.program_id(0) == 0)
    def _():
        knew[...] = jnp.zeros_like(knew)

    def copies(bb, sl):
        return (_page_copies(pt_ref, ckv_hbm, ckv_buf, sem_c, layer, bb, sl),
                _page_copies(pt_ref, kpe_hbm, kpe_buf, sem_k, layer, bb, sl))

    slot = _pipeline_pages(copies, n_pages)

    qc = qc_ref[0]
    ql = qc[:, 0:KV_LORA]
    qp = qp_ref[0]
    rows = qc.shape[0]

    def partial_softmax(scores, values):
        mc = scores[0]
        for s in scores[1:]:
            mc = jnp.maximum(mc, s)
        mc = _row_max(mc)
        lsum, acc = None, None
        for s, v in zip(scores, values):
            p = jnp.exp(s - mc)
            lsum = p if lsum is None else lsum + p
            pv = _dot(p.astype(BF16), v)
            acc = pv if acc is None else acc + pv
        return mc, lsum, acc

    knew[0:ts, 0:KV_LORA] = ckvn_ref[0]
    knew[0:ts, KV_LORA:2 * KV_LORA] = kpen_ref[0]
    kn = knew[...].astype(BF16)
    tok = lax.broadcasted_iota(jnp.int32, (rows, page), 0) % 8
    lane = lax.broadcasted_iota(jnp.int32, (rows, page), 1)
    st = jnp.where(lane <= tok, _dot_nt(qc, kn) * scale, NEG_INF)
    def chunk_scores(c0):
        scores, values = [], []
        for j in range(c0, min(c0 + _PAGE_CHUNK, n_pages)):
            ck = ckv_buf[slot, j].astype(BF16)
            scores.append((_dot_nt(ql, ck) + _dot(qp, kpe_buf[slot, j].astype(BF16))) * scale)
            values.append(ck)
        return scores, values

    starts = list(range(0, n_pages, _PAGE_CHUNK))
    pending = ([st], [kn[:, 0:KV_LORA]])
    parts = []
    for c0 in starts:
        nxt = chunk_scores(c0)
        parts.append(partial_softmax(*pending))
        pending = nxt
    parts.append(partial_softmax(*pending))
    m = parts[0][0]
    for mc, _, _ in parts[1:]:
        m = jnp.maximum(m, mc)
    l = jnp.zeros((rows, 1), F32)
    acc = jnp.zeros((rows, KV_LORA), F32)
    for mc, lsum, a in parts:
        w = jnp.exp(mc - m)
        l = l + w * _row_sum(lsum)
        acc = acc + w * a
    res = _dot((acc / l).astype(BF16), wuv_ref[...])
    headl = _lane_heads((8, BW), 1, V_A)
    out = jnp.zeros((8, BW), F32)
    for h in range(H_A):
        out = jnp.where(headl == h, res[h * 8:(h + 1) * 8, :], out)
    o_ref[0] = out


def _mla_sample(page_table, qc, qp, ckvn, kpen, cache_ckv, cache_kpeT, wuv_all, i):
    nb, n_pages = page_table.shape
    page = cache_ckv.shape[2]
    ts = ckvn.shape[1]
    grid_spec = pltpu.PrefetchScalarGridSpec(
        num_scalar_prefetch=1,
        grid=(nb,),
        in_specs=[
            pl.BlockSpec((1, 32, 2 * KV_LORA), lambda b, pt: (b, 0, 0)),
            pl.BlockSpec((1, 32, ROPE_A), lambda b, pt: (b, 0, 0)),
            pl.BlockSpec((1, ts, KV_LORA), lambda b, pt: (b, 0, 0)),
            pl.BlockSpec((1, ts, KV_LORA), lambda b, pt: (b, 0, 0)),
            pl.BlockSpec(memory_space=pl.ANY),
            pl.BlockSpec(memory_space=pl.ANY),
            pl.BlockSpec((None, KV_LORA, BW), lambda b, pt: (i, 0, 0)),
        ],
        out_specs=pl.BlockSpec((1, 8, BW), lambda b, pt: (b, 0, 0)),
        scratch_shapes=[
            pltpu.VMEM((2, n_pages, page, KV_LORA), F32),
            pltpu.VMEM((2, n_pages, ROPE_A, page), F32),
            pltpu.SemaphoreType.DMA((2,)),
            pltpu.SemaphoreType.DMA((2,)),
            pltpu.VMEM((page, 2 * KV_LORA), F32),
        ],
    )
    return pl.pallas_call(
        functools.partial(_mla_sample_kernel, layer=i, n_pages=n_pages, page=page, ts=ts),
        grid_spec=grid_spec,
        out_shape=jax.ShapeDtypeStruct((nb, 8, BW), F32),
        compiler_params=_cp("arbitrary"),
        name="mla_sample",
    )(page_table, qc, qp, ckvn, kpen, cache_ckv, cache_kpeT, wuv_all)


def _diff_sample_kernel(pt_ref, q_ref, kn_ref, vn_ref, bias_ref, dl_ref, sg_ref, k_hbm, v_hbm, o_ref,
                        k_buf, v_buf, sem_k, sem_v, s_scr, knew, vnew, *, layer, n_pages, page, ts, lam_init):
    scale = DH_D ** -0.5
    lam = _diff_lambda(dl_ref[...], lam_init)

    @pl.when(pl.program_id(0) == 0)
    def _():
        knew[...] = jnp.zeros_like(knew)
        vnew[...] = jnp.zeros_like(vnew)

    def copies(bb, sl):
        return (_page_copies(pt_ref, k_hbm, k_buf, sem_k, layer, bb, sl),
                _page_copies(pt_ref, v_hbm, v_buf, sem_v, layer, bb, sl))

    slot = _pipeline_pages(copies, n_pages)

    q8 = q_ref[0]
    lane = lax.broadcasted_iota(jnp.int32, (8, BW), 1)
    q64 = jnp.concatenate(
        [jnp.where(lane // DH_D == 2 * h + c, q8, 0.0) for c in range(2) for h in range(H_D)], axis=0).astype(BF16)
    rows = 64

    m_acc = jnp.full((rows, page), NEG_INF, F32)
    for j in range(n_pages):
        s = _dot(q64, k_buf[slot, j].astype(BF16)) * scale + bias_ref[:, j * page:(j + 1) * page]
        s_scr[:, j * page:(j + 1) * page] = s
        m_acc = jnp.maximum(m_acc, s)
    knew[0:ts, :] = kn_ref[0]
    vnew[0:ts, :] = vn_ref[0]
    tok = lax.broadcasted_iota(jnp.int32, (rows, page), 0) % 8
    lane_p = lax.broadcasted_iota(jnp.int32, (rows, page), 1)
    st = _dot_nt(q64, knew[...].astype(BF16)) * scale + bias_ref[:, n_pages * page:(n_pages + 1) * page]
    st = jnp.where(lane_p <= tok, st, NEG_INF)
    m = jnp.maximum(_row_max(m_acc), _row_max(st))

    l_acc = jnp.exp(st - m)
    acc = _dot(l_acc.astype(BF16), vnew[...].astype(BF16))
    for j in range(n_pages):
        p = jnp.exp(s_scr[:, j * page:(j + 1) * page] - m)
        l_acc = l_acc + p
        acc = acc + _dot_nt(p.astype(BF16), v_buf[slot, j].astype(BF16))
    o64 = acc / _row_sum(l_acc)
    acc = o64[0:32, :] - lam * o64[32:64, :]
    headl = _lane_heads((8, BW), 1, DV_D)
    out = jnp.zeros((8, BW), F32)
    for h in range(H_D):
        out = jnp.where(headl == h, acc[h * 8:(h + 1) * 8, :], out)
    ms = jnp.zeros_like(out)
    for h in range(H_D):
        mh = headl == h
        s = jnp.sum(jnp.where(mh, out * out, 0.0), axis=-1, keepdims=True) * (1.0 / DV_D)
        ms = jnp.where(mh, s, ms)
    o_ref[0] = out * lax.rsqrt(ms + EPS) * sg_ref[...] * (1.0 - lam_init)


def _diff_sample(page_table, q8, kn, vn, bias, dl, sg4, cache_kT, cache_vT, i, lam_init):
    nb, n_pages = page_table.shape
    page = cache_kT.shape[3]
    ts = kn.shape[1]
    grid_spec = pltpu.PrefetchScalarGridSpec(
        num_scalar_prefetch=1,
        grid=(nb,),
        in_specs=[
            pl.BlockSpec((1, 8, BW), lambda b, pt: (b, 0, 0)),
            pl.BlockSpec((1, ts, BW), lambda b, pt: (b, 0, 0)),
            pl.BlockSpec((1, ts, BW), lambda b, pt: (b, 0, 0)),
            pl.BlockSpec((64, (n_pages + 1) * page), lambda b, pt: (0, 0)),
            pl.BlockSpec((None, 4, DH_D), lambda b, pt: (i, 0, 0)),
            pl.BlockSpec((1, BW), lambda b, pt: (0, 0)),
            pl.BlockSpec(memory_space=pl.ANY),
            pl.BlockSpec(memory_space=pl.ANY),
        ],
        out_specs=pl.BlockSpec((1, 8, BW), lambda b, pt: (b, 0, 0)),
        scratch_shapes=[
            pltpu.VMEM((2, n_pages, BW, page), F32),
            pltpu.VMEM((2, n_pages, BW, page), F32),
            pltpu.SemaphoreType.DMA((2,)),
            pltpu.SemaphoreType.DMA((2,)),
            pltpu.VMEM((64, n_pages * page), F32),
            pltpu.VMEM((page, BW), F32),
            pltpu.VMEM((page, BW), F32),
        ],
    )
    return pl.pallas_call(
        functools.partial(_diff_sample_kernel, layer=i, n_pages=n_pages, page=page, ts=ts, lam_init=lam_init),
        grid_spec=grid_spec,
        out_shape=jax.ShapeDtypeStruct((nb, 8, BW), F32),
        compiler_params=_cp("arbitrary"),
        name="diff_sample",
    )(page_table, q8, kn, vn, bias, dl, sg4, cache_kT, cache_vT)


def _merge_kernel(x_ref, *refs, n_ptiles):
    br_p, br_s = refs[0:N_BRANCH], refs[N_BRANCH:2 * N_BRANCH]
    g_ref, wg_ref, wb_ref, wo_ref, o_ref = refs[2 * N_BRANCH:]
    is_prompt = pl.program_id(0) < n_ptiles
    x = x_ref[...]
    d = x.shape[1]
    un = _rms(x, g_ref[0:1, :]).astype(BF16)
    mix = jnp.zeros_like(x)
    for n in range(N_BRANCH):
        gate = _dot(un, wg_ref[:, n * d:(n + 1) * d])
        br = jnp.where(is_prompt, br_p[n][...], br_s[n][...])
        mix = mix + _sigmoid(gate) * _dot(br.astype(BF16), wb_ref[n])
    o_ref[...] = x + _rms(_dot(mix.astype(BF16), wo_ref[...]), g_ref[1:2, :])


def _merge(h, br_p, br_s, g2, wgates, wbr, wout, i, tm):
    n, d = h.shape
    n_ptiles = br_p[0].shape[0] // tm
    row_p = pl.BlockSpec((tm, BW), lambda t: (jnp.minimum(t, n_ptiles - 1), 0))
    row_s = pl.BlockSpec((tm, BW), lambda t: (jnp.maximum(t - n_ptiles, 0), 0))
    return pl.pallas_call(
        functools.partial(_merge_kernel, n_ptiles=n_ptiles),
        grid=(n // tm,),
        in_specs=[
            pl.BlockSpec((tm, d), lambda t: (t, 0))] + [row_p] * N_BRANCH + [row_s] * N_BRANCH + [
            pl.BlockSpec((2, d), lambda t: (0, 0)),
            pl.BlockSpec((None, d, N_BRANCH * d), lambda t: (i, 0, 0)),
            pl.BlockSpec((None, N_BRANCH, BW, d), lambda t: (i, 0, 0, 0)),
            pl.BlockSpec((None, d, d), lambda t: (i, 0, 0)),
        ],
        out_specs=pl.BlockSpec((tm, d), lambda t: (t, 0)),
        out_shape=jax.ShapeDtypeStruct((n, d), F32),
        compiler_params=_cp("parallel"),
        name="merge",
    )(h, *br_p, *br_s, g2, wgates, wbr, wout)


def _ple_kernel(x_ref, p_ref, g_ref, wg_ref, wi_ref, o_ref):
    x = x_ref[...]
    gate = _sigmoid(_dot(_rms(x, g_ref[0:1, :]).astype(BF16), wg_ref[...]))
    e = gate * _dot(p_ref[...].astype(BF16), wi_ref[...])
    o_ref[...] = x + _rms(e, g_ref[1:2, :])


def _ple(h, p, g2, wpg, wpi, i, tm):
    n, d = h.shape
    pd = p.shape[-1]
    return pl.pallas_call(
        _ple_kernel,
        grid=(n // tm,),
        in_specs=[
            pl.BlockSpec((tm, d), lambda t: (t, 0)),
            pl.BlockSpec((None, tm, pd), lambda t: (i, t, 0)),
            pl.BlockSpec((2, d), lambda t: (0, 0)),
            pl.BlockSpec((None, d, d), lambda t: (i, 0, 0)),
            pl.BlockSpec((None, pd, d), lambda t: (i, 0, 0)),
        ],
        out_specs=pl.BlockSpec((tm, d), lambda t: (t, 0)),
        out_shape=jax.ShapeDtypeStruct((n, d), F32),
        compiler_params=_cp("parallel"),
        name="ple",
    )(h, p, g2, wpg, wpi)


def _rope_table(pos):
    def cs(d, reps):
        half = d // 2
        freqs = ROPE_THETA ** (-jnp.arange(half, dtype=F32) / half)
        ang = pos.astype(F32)[:, None] * freqs[None, :]
        c, s = jnp.cos(ang), jnp.sin(ang)
        return jnp.tile(jnp.concatenate([c, c], -1), (1, reps)), jnp.tile(jnp.concatenate([-s, s], -1), (1, reps))
    cc, sc = cs(DK_C, H_C)
    ca, sa = cs(ROPE_A, 4)
    return jnp.concatenate([cc, sc, ca, sa], axis=-1)


def _t5_bucket(n):
    max_exact = T5_BUCKETS // 2
    nf = jnp.maximum(n, 1).astype(F32)
    large = max_exact + (jnp.log(nf / max_exact) / math.log(T5_MAX_DIST / max_exact)
                         * (T5_BUCKETS - max_exact)).astype(jnp.int32)
    large = jnp.minimum(large, T5_BUCKETS - 1)
    return jnp.where(n < max_exact, n, large)


def kernel(x_prompt, x_sample, cache_mla_ckv, cache_mla_kpe, cache_diff_k, cache_diff_v, state_lru_h, state_lru_conv, state_ret, page_table, p_prompt, p_sample, t5_bias, norm_g, w_ffn_gu, w_ffn_down, w_in, w_uq, g_q, g_kv, w_uk, w_uv, conv_w, conv_b, lru_wa, lru_ba, lru_wx, lru_bx, lru_lambda, ret_gn_g, diff_lambda, diff_subln_g, w_branch, w_out, w_ple_in, w_ple_gate):
    bp, t, d = x_prompt.shape
    bs, ts, _ = x_sample.shape
    depth = w_in.shape[0]
    n_pages = page_table.shape[1]
    page = cache_mla_ckv.shape[2]
    past_len = n_pages * page
    np_, ns_ = bp * t, bs * ts
    n = np_ + ns_
    ff = w_ffn_down.shape[2]

    tm = min(512, ns_)
    assert t % tm == 0 and ns_ % tm == 0
    fc = ff // 2 if (ff // 2) % LANES == 0 else ff
    tq = min(256, t)
    rc = min(256, t)
    assert ts <= 8 and t % tq == 0 and t % RET_CHUNK == 0 and t % rc == 0

    wgu_b = w_ffn_gu.astype(BF16)
    wdn_b = w_ffn_down.astype(BF16)
    w1 = _take_cols(w_in, _w1_columns()).astype(BF16)
    wgates = w_in[:, :, sum((256, 128, 32, 256, 256, 256, 256, 256, 256, 256, 256, 256)):].astype(BF16)
    hq = NOPE_A + ROPE_A
    nope_cols = np.concatenate([h * hq + np.arange(NOPE_A) for h in range(H_A)])
    pe_cols = np.concatenate([h * hq + NOPE_A + np.arange(ROPE_A) for h in range(H_A)])
    pes_cols = np.concatenate([h * hq + NOPE_A + _swap_halves(ROPE_A, ROPE_A) for h in range(H_A)])
    wuq = _take_cols(w_uq, np.concatenate([nope_cols, pe_cols, pes_cols])).astype(BF16)
    eye = jnp.eye(H_A, dtype=F32)
    wuk_t = jnp.transpose(w_uk, (0, 2, 3, 1))
    wuk_bd = (eye[None, :, None, :, None] * wuk_t[:, :, :, None, :]).reshape(depth, H_A * NOPE_A, H_A * KV_LORA).astype(BF16)
    wuv_t = jnp.transpose(w_uv, (0, 2, 1, 3))
    wuv_pad = (wuv_t[:, :, :, None, :] * eye[None, :, None, :, None]).reshape(depth, H_A, KV_LORA, BW).astype(BF16)
    wuv_all = w_uv.reshape(depth, KV_LORA, H_A * V_A).astype(BF16)
    eye_l = jnp.eye(LRU_BLOCKS, dtype=F32)
    bdiag = lambda w: (eye_l[None, :, None, :, None] * w[:, :, :, None, :]).reshape(depth, LRU_W, LRU_W).astype(BF16)
    wa_bd, wx_bd = bdiag(lru_wa), bdiag(lru_wx)
    lru_vec = jnp.stack([conv_b, lru_ba, lru_bx, lru_lambda], axis=1)
    wbr_b = w_branch.astype(BF16)
    wout_b = w_out.astype(BF16)
    wpg_b = w_ple_gate.astype(BF16)
    wpi_b = w_ple_in.astype(BF16)
    subln4 = jnp.tile(diff_subln_g, (1, H_D))
    gn_b = jnp.broadcast_to(ret_gn_g[:, :, None], (depth, BW, bs))

    pos_p = jnp.arange(t, dtype=jnp.int32)
    pos_s = past_len + jnp.arange(ts, dtype=jnp.int32)
    tab = jnp.concatenate([_rope_table(pos_p), _rope_table(jnp.tile(pos_s, tm // ts))], axis=0)
    tiles_per_seq = t // tm
    n_ptiles = np_ // tm
    tab_index = lambda i: jnp.where(i < n_ptiles, i % tiles_per_seq, tiles_per_seq)

    nq = t // tq
    r_ = jnp.arange(tq, dtype=jnp.int32)
    dist = jnp.maximum(jnp.arange(nq, dtype=jnp.int32)[:, None, None] * tq + r_[None, :, None] - r_[None, None, :], 0)
    bias_p = jnp.moveaxis(t5_bias[_t5_bucket(dist)], -1, 1).astype(F32)
    k_pos_s = jnp.arange(past_len + page, dtype=jnp.int32)
    dist_s = jnp.maximum(pos_s[:, None] - k_pos_s[None, :], 0)
    bias_s = jnp.moveaxis(t5_bias[_t5_bucket(dist_s)], -1, 0).astype(F32)
    bias_s = jnp.pad(bias_s, ((0, 0), (0, 8 - ts), (0, 0))).reshape(H_D * 8, past_len + page)
    bias_s = jnp.concatenate([bias_s, bias_s], axis=0)

    cache_kpeT = jnp.transpose(cache_mla_kpe, (0, 1, 3, 2))
    cache_dkT = jnp.transpose(cache_diff_k, (0, 1, 3, 4, 2)).reshape(depth, -1, BW, page)
    cache_dvT = jnp.transpose(cache_diff_v, (0, 1, 3, 4, 2)).reshape(depth, -1, BW, page)
    state_retT = jnp.transpose(state_ret, (0, 2, 3, 4, 1))
    conv_sT = jnp.transpose(state_lru_conv, (0, 2, 1, 3))

    h = jnp.concatenate([x_prompt.reshape(np_, d), x_sample.reshape(ns_, d)], axis=0)
    p_all = jnp.concatenate([p_prompt.reshape(depth, np_, -1), p_sample.reshape(depth, ns_, -1)], axis=1)

    def to_tb(a):
        return jnp.transpose(a.reshape(bs, ts, -1), (1, 0, 2))

    def to_twb(a):
        return jnp.transpose(a.reshape(bs, ts, -1), (1, 2, 0)).astype(F32)

    head_mask = (jnp.arange(H_A * ROPE_A)[None, :] // ROPE_A == jnp.arange(H_A)[:, None]).astype(BF16)

    outs = [[] for _ in range(14)]
    for i in range(depth):
        lam_init = 0.8 - 0.6 * math.exp(-0.3 * i)
        ng = norm_g[i]
        h = _ffn(h, ng[0:2], wgu_b, wdn_b, i, 0, tm, fc)
        (ckvn, kpe4, kcat, qlat, qpe, lgate, lx, rq, rk, rv, rg, dq, dk, dv, dkb, dvb) = _inproj(
            h, ng[2:3], w1, tab, g_q[i:i + 1], g_kv[i:i + 1], wuq, wuk_bd, i, tm, tab_index)
        kpe = kpe4[:, :ROPE_A]

        oa_p = _mla_prompt(qlat, qpe, kcat, wuv_pad, i, bp, t, tq)
        od_p = _diff_prompt(dq, dkb, dvb, bias_p, diff_lambda, diff_subln_g[i:i + 1], i, bp, t, tq, lam_init)
        oc_p, s_full = _ret_prompt(rq, rk, rv, rg, ret_gn_g[i:i + 1], i, bp, t)
        ob_p, hl_p = _lru_prompt(lx, lgate, conv_w, lru_vec, wa_bd, wx_bd, i, bp, t, rc)

        sl = slice(np_, n)
        pad_t = ((0, 0), (0, 0), (0, 8 - ts), (0, 0))
        ql_s = jnp.transpose(qlat[sl].reshape(bs, ts, H_A, KV_LORA), (0, 2, 1, 3))
        qp_rep = qpe[sl].reshape(bs, 1, ts, H_A * ROPE_A) * head_mask[None, :, None, :]
        qc_s = jnp.pad(jnp.concatenate([ql_s, qp_rep], axis=-1), pad_t).reshape(bs, H_A * 8, 2 * KV_LORA)
        qp_s = jnp.pad(jnp.transpose(qpe[sl].reshape(bs, ts, H_A, ROPE_A), (0, 2, 1, 3)),
                       pad_t).reshape(bs, H_A * 8, ROPE_A)
        oa_s = _mla_sample(page_table, qc_s, qp_s, ckvn[sl].reshape(bs, ts, KV_LORA),
                           kpe4[sl].reshape(bs, ts, KV_LORA), cache_mla_ckv, cache_kpeT, wuv_all, i)
        dq8 = jnp.pad(dq[sl].reshape(bs, ts, BW), ((0, 0), (0, 8 - ts), (0, 0))).astype(F32)
        od_s = _diff_sample(page_table, dq8, dk[sl].reshape(bs, ts, BW), dv[sl].reshape(bs, ts, BW), bias_s,
                            diff_lambda, subln4[i:i + 1], cache_dkT, cache_dvT, i, lam_init)
        oc_sT, s_new = _ret_sample(to_twb(rq[sl]), to_twb(rk[sl]), to_twb(rv[sl]), to_twb(rg[sl]),
                                   gn_b[i], state_retT, i)
        ob_s, hl_s = _lru_sample(to_tb(lx[sl]), to_tb(lgate[sl]), conv_sT, state_lru_h, conv_w, lru_vec,
                                 wa_bd, wx_bd, i)

        br_s = (oa_s[:, :ts].reshape(ns_, BW), jnp.transpose(ob_s, (1, 0, 2)).reshape(ns_, BW),
                jnp.transpose(oc_sT, (2, 0, 1)).reshape(ns_, BW), od_s[:, :ts].reshape(ns_, BW))
        h = _merge(h, (oa_p, ob_p, oc_p, od_p), br_s, ng[2:4], wgates, wbr_b, wout_b, i, tm)
        h = _ffn(h, ng[4:6], wgu_b, wdn_b, i, 1, tm, fc)
        h = _ple(h, p_all, ng[6:8], wpg_b, wpi_b, i, tm)

        lx_p = lx[:np_].reshape(bp, t, LRU_W)
        s5 = s_full.reshape(bp, H_C, DK_C, H_C, DV_C)
        ret_p = jnp.stack([s5[:, hh, :, hh, :] for hh in range(H_C)], axis=1)
        vals = [ckvn[:np_].reshape(bp, t, KV_LORA), kpe[:np_].reshape(bp, t, ROPE_A),
                dk[:np_].reshape(bp, t, H_D, 2 * DH_D), dv[:np_].reshape(bp, t, H_D, DV_D),
                hl_p.reshape(bp, LRU_W), lx_p[:, t - (CONV_W - 1):], ret_p,
                ckvn[sl].reshape(bs, ts, KV_LORA), kpe[sl].reshape(bs, ts, ROPE_A),
                dk[sl].reshape(bs, ts, H_D, 2 * DH_D), dv[sl].reshape(bs, ts, H_D, DV_D),
                hl_s, lx[sl].reshape(bs, ts, LRU_W)[:, ts - (CONV_W - 1):],
                jnp.transpose(s_new, (3, 0, 1, 2))]
        for lst, v in zip(outs, vals):
            lst.append(v)

    stacked = [jnp.stack(a) for a in outs]
    return (h[:np_].reshape(bp, t, d), h[np_:].reshape(bs, ts, d), *stacked)
```

```python
import functools
import math

import numpy as np
import jax
import jax.numpy as jnp
from jax import lax
from jax.experimental import pallas as pl
from jax.experimental.pallas import tpu as pltpu

F32 = jnp.float32
BF16 = jnp.bfloat16

N_BRANCH = 4
H_A, Q_LORA, KV_LORA, NOPE_A, ROPE_A, V_A = 4, 256, 128, 64, 32, 64
ROPE_THETA = 10000.0
LRU_W, LRU_BLOCKS, CONV_W, LRU_C = 256, 4, 4, 8.0
H_C, DK_C, DV_C, RET_CHUNK = 4, 64, 64, 128
H_D, DH_D, DV_D = 4, 32, 64
T5_BUCKETS, T5_MAX_DIST = 32, 128
EPS = 1e-6
NEG_INF = -1e30
BW = 256

VMEM_LIMIT = 56 * 1024 * 1024
LANES = 128

_NT = (((1,), (1,)), ((), ()))
_TN = (((0,), (0,)), ((), ()))


def _cp(*sem):
    return pltpu.CompilerParams(dimension_semantics=sem, vmem_limit_bytes=VMEM_LIMIT)


def _rms(x, g):
    return x * lax.rsqrt(jnp.mean(x * x, axis=-1, keepdims=True) + EPS) * g


def _dot(a, b):
    return jnp.dot(a, b, preferred_element_type=F32)


def _dot_nt(a, b):
    return lax.dot_general(a, b, _NT, preferred_element_type=F32)


def _sigmoid(x):
    return 1.0 / (1.0 + jnp.exp(-x))


def _ffn_kernel(x_ref, g_ref, wg_ref, wu_ref, wd_ref, o_ref, xn_ref, acc_ref):
    j = pl.program_id(1)

    @pl.when(j == 0)
    def _():
        xn_ref[...] = _rms(x_ref[...], g_ref[0:1, :]).astype(BF16)
        acc_ref[...] = jnp.zeros_like(acc_ref)

    xn = xn_ref[...]
    g = _dot(xn, wg_ref[...])
    u = _dot(xn, wu_ref[...])
    a = (g * _sigmoid(g) * u).astype(BF16)
    acc_ref[...] += _dot(a, wd_ref[...])

    @pl.when(j == pl.num_programs(1) - 1)
    def _():
        o_ref[...] = x_ref[...] + 0.5 * _rms(acc_ref[...], g_ref[1:2, :])


def _ffn(h, g2, wgu, wdn, i, k, tm, fc):
    n, d = h.shape
    ff = wdn.shape[2]
    nf = ff // fc
    return pl.pallas_call(
        _ffn_kernel,
        grid=(n // tm, nf),
        in_specs=[
            pl.BlockSpec((tm, d), lambda t, j: (t, 0)),
            pl.BlockSpec((2, d), lambda t, j: (0, 0)),
            pl.BlockSpec((None, None, d, fc), lambda t, j: (i, k, 0, j)),
            pl.BlockSpec((None, None, d, fc), lambda t, j: (i, k, 0, nf + j)),
            pl.BlockSpec((None, None, fc, d), lambda t, j: (i, k, j, 0)),
        ],
        out_specs=pl.BlockSpec((tm, d), lambda t, j: (t, 0)),
        out_shape=jax.ShapeDtypeStruct((n, d), F32),
        scratch_shapes=[pltpu.VMEM((tm, d), BF16), pltpu.VMEM((tm, d), F32)],
        compiler_params=_cp("parallel", "arbitrary"),
        name="ffn",
    )(h, g2, wgu, wgu, wdn)


_C_CQ, _C_CKV, _C_KPE, _C_KPES, _C_LG, _C_LX = 0, 256, 384, 512, 640, 896
_C_RQ, _C_RK, _C_RV, _C_RG, _C_DQ, _C_DK, _C_DV, _C_RQS, _C_RKS = 1152, 1408, 1664, 1920, 2176, 2432, 2688, 2944, 3200
_W1_COLS = 3456


def _swap_halves(width, group):
    idx = np.arange(width)
    half = group // 2
    return np.where(idx % group < half, idx + half, idx - half)


def _w1_columns():
    o_cq, o_ckv, o_kpe, o_lg, o_lx = 0, 256, 384, 416, 672
    o_rq, o_rk, o_rv, o_rg, o_dq, o_dk, o_dv = 928, 1184, 1440, 1696, 1952, 2208, 2464
    r = np.arange
    kpe = o_kpe + np.tile(r(ROPE_A), 4)
    kpe_s = o_kpe + np.tile(_swap_halves(ROPE_A, ROPE_A), 4)
    cols = np.concatenate([
        o_cq + r(256), o_ckv + r(128), kpe, kpe_s, o_lg + r(256), o_lx + r(256),
        o_rq + r(256), o_rk + r(256), o_rv + r(256), o_rg + r(256),
        o_dq + r(256), o_dk + r(256), o_dv + r(256),
        o_rq + _swap_halves(256, DK_C), o_rk + _swap_halves(256, DK_C)])
    assert cols.shape[0] == _W1_COLS
    return cols


def _take_cols(w, cols):
    cols = np.asarray(cols)
    cuts = np.flatnonzero(np.diff(cols) != 1) + 1
    runs = np.split(cols, cuts)
    return jnp.concatenate([w[..., int(r[0]):int(r[-1]) + 1] for r in runs], axis=-1)


def _inproj_kernel(x_ref, g_ref, w1_ref, tab_ref, gq_ref, gkv_ref, wuq_ref, wuk_ref,
                   ckv_o, kpe_o, kcat_o, qlat_o, qpe_o, lg_o, lx_o, rq_o, rk_o, rv_o, rg_o,
                   dq_o, dk_o, dv_o, dkb_o, dvb_o):
    un = _rms(x_ref[...], g_ref[...]).astype(BF16)
    z = _dot(un, w1_ref[...])
    cosc, sinc = tab_ref[:, 0:256], tab_ref[:, 256:512]
    cosa, sina = tab_ref[:, 512:640], tab_ref[:, 640:768]

    cqn = _rms(z[:, _C_CQ:_C_CQ + 256], gq_ref[...]).astype(BF16)
    q = _dot(cqn, wuq_ref[...])
    qpe_o[...] = (q[:, 256:384] * cosa + q[:, 384:512] * sina).astype(BF16)
    qlat_o[...] = _dot(q[:, 0:256].astype(BF16), wuk_ref[...]).astype(BF16)

    ckvn = _rms(z[:, _C_CKV:_C_CKV + 128], gkv_ref[...])
    kpe4 = z[:, _C_KPE:_C_KPE + 128] * cosa + z[:, _C_KPES:_C_KPES + 128] * sina
    ckv_o[...] = ckvn
    kpe_o[...] = kpe4
    kcat_o[:, 0:128] = ckvn.astype(BF16)
    kcat_o[:, 128:256] = kpe4.astype(BF16)

    lg_o[...] = z[:, _C_LG:_C_LG + 256]
    lx_o[...] = z[:, _C_LX:_C_LX + 256]
    rq_o[...] = (z[:, _C_RQ:_C_RQ + 256] * cosc + z[:, _C_RQS:_C_RQS + 256] * sinc).astype(BF16)
    rk_o[...] = (z[:, _C_RK:_C_RK + 256] * cosc + z[:, _C_RKS:_C_RKS + 256] * sinc) * (DK_C ** -0.5)
    rv_o[...] = z[:, _C_RV:_C_RV + 256].astype(BF16)
    rg_o[...] = z[:, _C_RG:_C_RG + 256]
    dq_o[...] = z[:, _C_DQ:_C_DQ + 256].astype(BF16)
    dk = z[:, _C_DK:_C_DK + 256]
    dv = z[:, _C_DV:_C_DV + 256]
    dk_o[...] = dk
    dv_o[...] = dv
    dkb_o[...] = dk.astype(BF16)
    dvb_o[...] = dv.astype(BF16)


def _inproj(h, g, w1, tab, gq, gkv, wuq, wuk, i, tm, tab_index):
    n, d = h.shape
    widths = [(128, F32), (128, F32), (256, BF16), (512, BF16), (128, BF16), (256, F32), (256, F32),
              (256, BF16), (256, F32), (256, BF16), (256, F32), (256, BF16), (256, F32), (256, F32),
              (256, BF16), (256, BF16)]
    return pl.pallas_call(
        _inproj_kernel,
        grid=(n // tm,),
        in_specs=[
            pl.BlockSpec((tm, d), lambda t: (t, 0)),
            pl.BlockSpec((1, d), lambda t: (0, 0)),
            pl.BlockSpec((None, d, _W1_COLS), lambda t: (i, 0, 0)),
            pl.BlockSpec((tm, 768), lambda t: (tab_index(t), 0)),
            pl.BlockSpec((1, Q_LORA), lambda t: (0, 0)),
            pl.BlockSpec((1, KV_LORA), lambda t: (0, 0)),
            pl.BlockSpec((None, Q_LORA, 512), lambda t: (i, 0, 0)),
            pl.BlockSpec((None, 256, 512), lambda t: (i, 0, 0)),
        ],
        out_specs=[pl.BlockSpec((tm, w), lambda t: (t, 0)) for w, _ in widths],
        out_shape=[jax.ShapeDtypeStruct((n, w), dt) for w, dt in widths],
        compiler_params=_cp("parallel"),
        name="inproj",
    )(h, g, w1, tab, gq, gkv, wuq, wuk)


def _mla_prompt_kernel(qlat_ref, qpe_ref, k_ref, wuv_ref, o_ref, q4_ref, *, tq):
    qi = pl.program_id(1)
    scale = (NOPE_A + ROPE_A) ** -0.5
    qpe = qpe_ref[...].astype(F32)
    lane = lax.broadcasted_iota(jnp.int32, (tq, LANES), 1)
    for h in range(H_A):
        q4_ref[h * tq:(h + 1) * tq, 0:KV_LORA] = qlat_ref[:, h * KV_LORA:(h + 1) * KV_LORA]
        q4_ref[h * tq:(h + 1) * tq, KV_LORA:2 * KV_LORA] = jnp.where(lane // ROPE_A == h, qpe, 0.0).astype(BF16)
    row = lax.broadcasted_iota(jnp.int32, (tq, tq), 0)
    col = lax.broadcasted_iota(jnp.int32, (tq, tq), 1)

    def keys(j):
        return k_ref[pl.ds(pl.multiple_of(j * tq, tq), tq), :]

    def scores(j):
        return _dot_nt(q4_ref[...], keys(j))

    def update(j, s_all, state, masked):
        ms, ls, accs = state
        v = keys(j)[:, 0:KV_LORA]
        ms2, ls2, accs2 = [], [], []
        for h in range(H_A):
            s = s_all[h * tq:(h + 1) * tq, :] * scale
            if masked:
                s = jnp.where(col <= row, s, NEG_INF)
            m_new = jnp.maximum(ms[h], jnp.max(s, axis=-1, keepdims=True))
            alpha = jnp.exp(ms[h] - m_new)
            p = jnp.exp(s - m_new)
            ls2.append(alpha * ls[h] + jnp.sum(p, axis=-1, keepdims=True))
            accs2.append(alpha * accs[h] + _dot(p.astype(BF16), v))
            ms2.append(m_new)
        return tuple(ms2), tuple(ls2), tuple(accs2)

    def step(j, state, masked):
        return update(j, scores(j), state, masked)

    init = (tuple(jnp.full((tq, 1), NEG_INF, F32) for _ in range(H_A)),
            tuple(jnp.zeros((tq, 1), F32) for _ in range(H_A)),
            tuple(jnp.zeros((tq, KV_LORA), F32) for _ in range(H_A)))
    state = lax.fori_loop(0, qi, functools.partial(step, masked=False), init)
    _, ls, accs = step(qi, state, True)
    out = jnp.zeros((tq, BW), F32)
    for h in range(H_A):
        out = out + _dot((accs[h] / ls[h]).astype(BF16), wuv_ref[h])
    o_ref[...] = out


def _mla_prompt(qlat, qpe, kcat, wuvp, i, b, t, tq):
    nq = t // tq
    return pl.pallas_call(
        functools.partial(_mla_prompt_kernel, tq=tq),
        grid=(b, nq),
        in_specs=[
            pl.BlockSpec((tq, 512), lambda bb, qi: (bb * nq + qi, 0)),
            pl.BlockSpec((tq, 128), lambda bb, qi: (bb * nq + qi, 0)),
            pl.BlockSpec((t, 256), lambda bb, qi: (bb, 0)),
            pl.BlockSpec((None, H_A, KV_LORA, BW), lambda bb, qi: (i, 0, 0, 0)),
        ],
        out_specs=pl.BlockSpec((tq, BW), lambda bb, qi: (bb * nq + qi, 0)),
        out_shape=jax.ShapeDtypeStruct((b * t, BW), F32),
        scratch_shapes=[pltpu.VMEM((H_A * tq, 2 * KV_LORA), BF16)],
        compiler_params=_cp("parallel", "arbitrary"),
        name="mla_prompt",
    )(qlat, qpe, kcat, wuvp)


def _t5_expand_kernel(tab_ref, bkt_ref, o_ref):
    bkt = bkt_ref[...]
    accs = [jnp.full(bkt.shape, tab_ref[T5_BUCKETS - 1, h], F32) for h in range(H_D)]
    for k in range(T5_BUCKETS - 1):
        hit = bkt == k
        accs = [jnp.where(hit, tab_ref[k, h], accs[h]) for h in range(H_D)]
    for h in range(H_D):
        o_ref[h] = accs[h]


def _t5_expand(table, bucket, block_rows):
    r, c = bucket.shape
    return pl.pallas_call(
        _t5_expand_kernel,
        grid=(r // block_rows,),
        in_specs=[pl.BlockSpec(memory_space=pltpu.SMEM), pl.BlockSpec((block_rows, c), lambda g: (g, 0))],
        out_specs=pl.BlockSpec((H_D, block_rows, c), lambda g: (0, g, 0)),
        out_shape=jax.ShapeDtypeStruct((H_D, r, c), F32),
        compiler_params=_cp("parallel"),
        name="t5_bias",
    )(table, bucket)


def _diff_lambda(dl, lam_init):
    a = jnp.sum(dl[0:1, :] * dl[1:2, :], axis=-1, keepdims=True)
    b = jnp.sum(dl[2:3, :] * dl[3:4, :], axis=-1, keepdims=True)
    return jnp.exp(a) - jnp.exp(b) + lam_init


def _diff_prompt_kernel(dq_ref, k_ref, v_ref, bias_ref, dl_ref, sg_ref, o_ref, q8_ref, *, tq, lam_init):
    qi = pl.program_id(1)
    scale = DH_D ** -0.5
    lam = _diff_lambda(dl_ref[...], lam_init)
    q = dq_ref[...].astype(F32)
    lane = lax.broadcasted_iota(jnp.int32, (tq, BW), 1)
    for hc in range(2 * H_D):
        q8_ref[hc * tq:(hc + 1) * tq, :] = jnp.where(lane // DH_D == hc, q, 0.0).astype(BF16)
    row = lax.broadcasted_iota(jnp.int32, (tq, tq), 0)
    col = lax.broadcasted_iota(jnp.int32, (tq, tq), 1)

    def scores(j):
        return _dot_nt(q8_ref[...], k_ref[pl.ds(pl.multiple_of(j * tq, tq), tq), :])

    def update(j, s_all, state, masked):
        ms, ls, accs = state
        v = v_ref[pl.ds(pl.multiple_of(j * tq, tq), tq), :]
        ms2, ls2, accs2 = [], [], []
        for hc in range(2 * H_D):
            h = hc // 2
            s = s_all[hc * tq:(hc + 1) * tq, :] * scale + bias_ref[h, qi - j]
            if masked:
                s = jnp.where(col <= row, s, NEG_INF)
            m_new = jnp.maximum(ms[hc], jnp.max(s, axis=-1, keepdims=True))
            alpha = jnp.exp(ms[hc] - m_new)
            p = jnp.exp(s - m_new)
            ls2.append(alpha * ls[hc] + jnp.sum(p, axis=-1, keepdims=True))
            vh = v[:, (h // 2) * LANES:(h // 2 + 1) * LANES]
            accs2.append(alpha * accs[hc] + _dot(p.astype(BF16), vh))
            ms2.append(m_new)
        return tuple(ms2), tuple(ls2), tuple(accs2)

    n = 2 * H_D
    init = (tuple(jnp.full((tq, 1), NEG_INF, F32) for _ in range(n)),
            tuple(jnp.zeros((tq, 1), F32) for _ in range(n)),
            tuple(jnp.zeros((tq, LANES), F32) for _ in range(n)))

    def step(j, state, masked):
        return update(j, scores(j), state, masked)

    state = lax.fori_loop(0, qi, functools.partial(step, masked=False), init)
    _, ls, accs = step(qi, state, True)
    sg = sg_ref[...]
    for h in range(H_D):
        o = accs[2 * h] / ls[2 * h] - lam * (accs[2 * h + 1] / ls[2 * h + 1])
        oh = o[:, (h % 2) * DV_D:(h % 2 + 1) * DV_D]
        o_ref[:, h * DV_D:(h + 1) * DV_D] = _rms(oh, sg) * (1.0 - lam_init)


def _diff_prompt(dq, dkb, dvb, bias, dl, sg, i, b, t, tq, lam_init):
    nq = t // tq
    return pl.pallas_call(
        functools.partial(_diff_prompt_kernel, tq=tq, lam_init=lam_init),
        grid=(b, nq),
        in_specs=[
            pl.BlockSpec((tq, BW), lambda bb, qi: (bb * nq + qi, 0)),
            pl.BlockSpec((t, BW), lambda bb, qi: (bb, 0)),
            pl.BlockSpec((t, BW), lambda bb, qi: (bb, 0)),
            pl.BlockSpec((H_D, nq, tq, tq), lambda bb, qi: (0, 0, 0, 0)),
            pl.BlockSpec((None, 4, DH_D), lambda bb, qi: (i, 0, 0)),
            pl.BlockSpec((1, DV_D), lambda bb, qi: (0, 0)),
        ],
        out_specs=pl.BlockSpec((tq, BW), lambda bb, qi: (bb * nq + qi, 0)),
        out_shape=jax.ShapeDtypeStruct((b * t, BW), F32),
        scratch_shapes=[pltpu.VMEM((2 * H_D * tq, BW), BF16)],
        compiler_params=_cp("parallel", "arbitrary"),
        name="diff_prompt",
    )(dq, dkb, dvb, bias, dl, sg)


def _ret_log_gamma(h):
    return math.log1p(-(2.0 ** (-5.0 - h)))


def _lane_heads(shape, axis, width):
    return lax.broadcasted_iota(jnp.int32, shape, axis) // width


def _per_head(idx, vals):
    out = jnp.full(idx.shape, vals[-1], F32)
    for h in range(len(vals) - 2, -1, -1):
        out = jnp.where(idx == h, vals[h], out)
    return out


def _group_norm_lanes(o, headl, n_heads, width):
    mu = jnp.zeros_like(o)
    for h in range(n_heads):
        mh = headl == h
        s = jnp.sum(jnp.where(mh, o, 0.0), axis=-1, keepdims=True) * (1.0 / width)
        mu = jnp.where(mh, s, mu)
    d = o - mu
    var = jnp.zeros_like(o)
    for h in range(n_heads):
        mh = headl == h
        s = jnp.sum(jnp.where(mh, d * d, 0.0), axis=-1, keepdims=True) * (1.0 / width)
        var = jnp.where(mh, s, var)
    return d * lax.rsqrt(var + EPS)


def _ret_prompt_kernel(q_ref, k_ref, v_ref, g_ref, gn_ref, o_ref, s_ref, s_scr, *, n_chunks):
    c = RET_CHUNK
    lgs = [_ret_log_gamma(h) for h in range(H_C)]
    headl = _lane_heads((c, BW), 1, DK_C)
    lgl = _per_head(headl, lgs)
    idx = lax.broadcasted_iota(jnp.int32, (c, BW), 0).astype(F32)
    qdec = jnp.exp((idx + 1.0) * lgl)
    kdec = jnp.exp((c - 1.0 - idx) * lgl)
    cdec = jnp.exp(float(c) * _per_head(_lane_heads((BW, BW), 0, DK_C), lgs))
    ri = lax.broadcasted_iota(jnp.int32, (c, c), 0)
    ci = lax.broadcasted_iota(jnp.int32, (c, c), 1)
    dif = (ri - ci).astype(F32)
    intra = [jnp.where(dif >= 0, jnp.exp(jnp.maximum(dif, 0.0) * lgs[h]), 0.0) for h in range(H_C)]
    bd = _lane_heads((BW, BW), 0, DK_C) == _lane_heads((BW, BW), 1, DV_C)
    s_scr[...] = jnp.zeros_like(s_scr)

    def chunk(ic, _):
        rows = pl.ds(pl.multiple_of(ic * c, c), c)
        q = q_ref[rows, :]
        k = k_ref[rows, :]
        v = v_ref[rows, :]
        s = s_scr[...]
        o = _dot(q, s.astype(BF16)) * qdec
        for h in range(H_C):
            mh = headl == h
            att = _dot_nt(q, jnp.where(mh, k, 0.0).astype(BF16)) * intra[h]
            o = o + jnp.where(mh, _dot(att.astype(BF16), v), 0.0)
        kd = (k * kdec).astype(BF16)
        upd = lax.dot_general(kd, v, _TN, preferred_element_type=F32)
        s_scr[...] = cdec * s + jnp.where(bd, upd, 0.0)
        y = _group_norm_lanes(o, headl, H_C, DV_C) * gn_ref[...]
        g = g_ref[rows, :]
        o_ref[rows, :] = g * _sigmoid(g) * y
        return 0

    lax.fori_loop(0, n_chunks, chunk, 0)
    s_ref[...] = s_scr[...]


def _ret_prompt(rq, rk, rv, rg, gn, i, b, t):
    return pl.pallas_call(
        functools.partial(_ret_prompt_kernel, n_chunks=t // RET_CHUNK),
        grid=(b,),
        in_specs=[pl.BlockSpec((t, BW), lambda bb: (bb, 0))] * 4 + [pl.BlockSpec((1, BW), lambda bb: (0, 0))],
        out_specs=[pl.BlockSpec((t, BW), lambda bb: (bb, 0)), pl.BlockSpec((None, BW, BW), lambda bb: (bb, 0, 0))],
        out_shape=[jax.ShapeDtypeStruct((b * t, BW), F32), jax.ShapeDtypeStruct((b, BW, BW), F32)],
        scratch_shapes=[pltpu.VMEM((BW, BW), F32)],
        compiler_params=_cp("parallel"),
        name="ret_prompt",
    )(rq, rk, rv, rg, gn)


def _lru_gates(conv, wa, wx, vec):
    cb = conv.astype(BF16)
    r = _sigmoid(_dot(cb, wa) + vec[1:2, :])
    ig = _sigmoid(_dot(cb, wx) + vec[2:3, :])
    nl = -vec[3:4, :]
    sp = jnp.maximum(nl, 0.0) + jnp.log1p(jnp.exp(-jnp.abs(nl)))
    log_a = -LRU_C * r * sp
    a = jnp.exp(log_a)
    u = jnp.sqrt(-jnp.tanh(log_a) * (a * a + 1.0)) * (ig * conv)
    return a, u


def _gelu(x):
    return 0.5 * x * (1.0 + jnp.tanh(math.sqrt(2.0 / math.pi) * (x + 0.044715 * (x * x * x))))


def _lru_prompt_kernel(x_ref, g_ref, cw_ref, vec_ref, wa_ref, wx_ref, y_ref, hl_ref,
                       xp, a_s, u_s, h_s, *, rc):
    ic = pl.program_id(1)
    pad = 8

    @pl.when(ic == 0)
    def _():
        xp[0:pad, :] = jnp.zeros((pad, LRU_W), F32)
        h_s[...] = jnp.zeros_like(h_s)

    xp[pad:pad + rc, :] = x_ref[...]
    vec = vec_ref[...]
    conv = vec[0:1, :] + cw_ref[3:4, :] * xp[pad:pad + rc, :]
    for j in range(CONV_W - 1):
        k = CONV_W - 1 - j
        conv = conv + cw_ref[j:j + 1, :] * xp[pad - k:pad - k + rc, :]
    xp[0:pad, :] = xp[rc:rc + pad, :]

    a, u = _lru_gates(conv, wa_ref[...], wx_ref[...], vec)
    a_s[0:pad, :] = jnp.ones((pad, LRU_W), F32)
    u_s[0:pad, :] = jnp.zeros((pad, LRU_W), F32)
    a_s[pad:pad + rc, :] = a
    u_s[pad:pad + rc, :] = u
    tmod = lax.broadcasted_iota(jnp.int32, (rc, LRU_W), 0) % pad
    for d in (1, 2, 4):
        a_cur, u_cur = a_s[pad:pad + rc, :], u_s[pad:pad + rc, :]
        a_sh, u_sh = a_s[pad - d:pad - d + rc, :], u_s[pad - d:pad - d + rc, :]
        m = tmod >= d
        a_s[pad:pad + rc, :] = jnp.where(m, a_cur * a_sh, a_cur)
        u_s[pad:pad + rc, :] = jnp.where(m, a_cur * u_sh + u_cur, u_cur)
    h = h_s[...]
    for blk in range(rc // pad):
        lo = pad + blk * pad
        hs = a_s[lo:lo + pad, :] * h + u_s[lo:lo + pad, :]
        u_s[lo:lo + pad, :] = hs
        h = hs[pad - 1:pad, :]
    h_s[...] = h
    hl_ref[...] = h
    y_ref[...] = u_s[pad:pad + rc, :] * _gelu(g_ref[...])


def _lru_prompt(lx, lgate, cw, vec, wa, wx, i, b, t, rc):
    nc = t // rc
    return pl.pallas_call(
        functools.partial(_lru_prompt_kernel, rc=rc),
        grid=(b, nc),
        in_specs=[
            pl.BlockSpec((rc, LRU_W), lambda bb, c: (bb * nc + c, 0)),
            pl.BlockSpec((rc, LRU_W), lambda bb, c: (bb * nc + c, 0)),
            pl.BlockSpec((None, CONV_W, LRU_W), lambda bb, c: (i, 0, 0)),
            pl.BlockSpec((None, 4, LRU_W), lambda bb, c: (i, 0, 0)),
            pl.BlockSpec((None, LRU_W, LRU_W), lambda bb, c: (i, 0, 0)),
            pl.BlockSpec((None, LRU_W, LRU_W), lambda bb, c: (i, 0, 0)),
        ],
        out_specs=[pl.BlockSpec((rc, LRU_W), lambda bb, c: (bb * nc + c, 0)),
                   pl.BlockSpec((None, 1, LRU_W), lambda bb, c: (bb, 0, 0))],
        out_shape=[jax.ShapeDtypeStruct((b * t, LRU_W), F32), jax.ShapeDtypeStruct((b, 1, LRU_W), F32)],
        scratch_shapes=[pltpu.VMEM((rc + 8, LRU_W), F32), pltpu.VMEM((rc + 8, LRU_W), F32),
                        pltpu.VMEM((rc + 8, LRU_W), F32), pltpu.VMEM((1, LRU_W), F32)],
        compiler_params=_cp("parallel", "arbitrary"),
        name="lru_prompt",
    )(lx, lgate, cw, vec, wa, wx)


def _lru_sample_kernel(x_ref, g_ref, cb_ref, h0_ref, cw_ref, vec_ref, wa_ref, wx_ref, y_ref, hl_ref, *, t):
    vec = vec_ref[...]
    xs = [cb_ref[j] for j in range(CONV_W - 1)] + [x_ref[j] for j in range(t)]
    h = h0_ref[...]
    for n in range(t):
        conv = vec[0:1, :]
        for j in range(CONV_W):
            conv = conv + cw_ref[j:j + 1, :] * xs[n + j]
        a, u = _lru_gates(conv, wa_ref[...], wx_ref[...], vec)
        h = a * h + u
        y_ref[n] = h * _gelu(g_ref[n])
    hl_ref[...] = h


def _lru_sample(xs, gs, cb, h0, cw, vec, wa, wx, i):
    t, b, w = xs.shape
    full = lambda *shape: pl.BlockSpec(shape, lambda g: (0,) * len(shape))
    lay = lambda *shape: pl.BlockSpec((None,) + shape, lambda g: (i,) + (0,) * len(shape))
    return pl.pallas_call(
        functools.partial(_lru_sample_kernel, t=t),
        grid=(1,),
        in_specs=[full(t, b, w), full(t, b, w), lay(CONV_W - 1, b, w), lay(b, w),
                  lay(CONV_W, w), lay(4, w), lay(w, w), lay(w, w)],
        out_specs=[full(t, b, w), full(b, w)],
        out_shape=[jax.ShapeDtypeStruct((t, b, w), F32), jax.ShapeDtypeStruct((b, w), F32)],
        compiler_params=_cp("arbitrary"),
        name="lru_sample",
    )(xs, gs, cb, h0, cw, vec, wa, wx)


def _ret_sample_kernel(q_ref, k_ref, v_ref, g_ref, gn_ref, s_ref, o_ref, sn_ref, *, t):
    for h in range(H_C):
        lg = _ret_log_gamma(h)
        lo = h * DK_C
        qs = [q_ref[n, lo:lo + DK_C, :] for n in range(t)]
        ks = [k_ref[n, lo:lo + DK_C, :] for n in range(t)]
        vs = [v_ref[n, lo:lo + DV_C, :] for n in range(t)]
        os_ = []
        for n in range(t):
            o = jnp.zeros_like(vs[0])
            for m in range(n + 1):
                att = jnp.sum(qs[n] * ks[m], axis=0, keepdims=True) * math.exp((n - m) * lg)
                o = o + att * vs[m]
            os_.append(o)
        kdec = [math.exp((t - 1 - m) * lg) for m in range(t)]
        qdec = [math.exp((n + 1) * lg) for n in range(t)]
        cdec = math.exp(t * lg)

        def body(d, carry, h=h, lo=lo, vs=vs, kdec=kdec, cdec=cdec):
            srow = s_ref[h, d]
            new = cdec * srow
            out = []
            for n in range(t):
                qd = q_ref[n, pl.ds(lo + d, 1), :]
                kd = k_ref[n, pl.ds(lo + d, 1), :]
                out.append(carry[n] + qd * srow)
                new = new + (kd * kdec[n]) * vs[n]
            sn_ref[h, d] = new
            return tuple(out)

        cross = lax.fori_loop(0, DK_C, body, tuple(jnp.zeros_like(vs[0]) for _ in range(t)))
        gn = gn_ref[lo:lo + DV_C, :]
        for n in range(t):
            o = os_[n] + cross[n] * qdec[n]
            mu = jnp.mean(o, axis=0, keepdims=True)
            d = o - mu
            var = jnp.mean(d * d, axis=0, keepdims=True)
            y = d * lax.rsqrt(var + EPS) * gn
            g = g_ref[n, lo:lo + DV_C, :]
            o_ref[n, lo:lo + DV_C, :] = g * _sigmoid(g) * y


def _ret_sample(qT, kT, vT, gT, gnb, state, i):
    t, w, b = qT.shape
    full = lambda *shape: pl.BlockSpec(shape, lambda g: (0,) * len(shape))
    return pl.pallas_call(
        functools.partial(_ret_sample_kernel, t=t),
        grid=(1,),
        in_specs=[full(t, w, b)] * 4 + [full(w, b),
                  pl.BlockSpec((None, H_C, DK_C, DV_C, b), lambda g: (i, 0, 0, 0, 0))],
        out_specs=[full(t, w, b), full(H_C, DK_C, DV_C, b)],
        out_shape=[jax.ShapeDtypeStruct((t, w, b), F32), jax.ShapeDtypeStruct((H_C, DK_C, DV_C, b), F32)],
        compiler_params=_cp("arbitrary"),
        name="ret_sample",
    )(qT, kT, vT, gT, gnb, state)


_DMA_UNROLL = 8
_PAGE_CHUNK = 8


def _page_copies(pt_ref, hbm, buf, sem, layer, b, slot):
    def at(j):
        return pltpu.make_async_copy(hbm.at[layer, pt_ref[b, j]], buf.at[slot, j], sem.at[slot])
    return at


def _start_pages(at, n_pages):
    def body(j, _):
        at(j).start()
        return 0
    lax.fori_loop(0, n_pages, body, 0, unroll=_DMA_UNROLL)


def _wait_pages(at, n_pages):
    def body(j, _):
        at(j).wait()
        return 0
    lax.fori_loop(0, n_pages, body, 0, unroll=_DMA_UNROLL)


def _prefetch_pages(copies, n_pages):
    b = pl.program_id(0)
    slot = b % 2

    @pl.when(b == 0)
    def _():
        for at in copies(0, 0):
            _start_pages(at, n_pages)

    @pl.when(b + 1 < pl.num_programs(0))
    def _():
        for at in copies(b + 1, 1 - slot):
            _start_pages(at, n_pages)

    return slot


def _row_max(x):
    return jnp.max(x, axis=-1, keepdims=True)


def _row_sum(x):
    return jnp.sum(x, axis=-1, keepdims=True)


def _mla_decode(qc_ref, qp_ref, ckvn_ref, kpen_ref, wuv_ref, o_ref, ckv_buf, kpe_buf, knew, slot,
                *, n_pages, page, ts):
    scale = (NOPE_A + ROPE_A) ** -0.5
    qc = qc_ref[0]
    ql = qc[:, 0:KV_LORA]
    qp = qp_ref[0]
    rows = qc.shape[0]

    def partial_softmax(scores, values):
        mc = scores[0]
        for s in scores[1:]:
            mc = jnp.maximum(mc, s)
        mc = _row_max(mc)
        lsum, acc = None, None
        for s, v in zip(scores, values):
            p = jnp.exp(s - mc)
            lsum = p if lsum is None else lsum + p
            pv = _dot(p.astype(BF16), v)
            acc = pv if acc is None else acc + pv
        return mc, lsum, acc

    knew[0:ts, 0:KV_LORA] = ckvn_ref[0]
    knew[0:ts, KV_LORA:2 * KV_LORA] = kpen_ref[0]
    kn = knew[...].astype(BF16)
    tok = lax.broadcasted_iota(jnp.int32, (rows, page), 0) % 8
    lane = lax.broadcasted_iota(jnp.int32, (rows, page), 1)
    st = jnp.where(lane <= tok, _dot_nt(qc, kn) * scale, NEG_INF)
    def chunk_scores(c0):
        scores, values = [], []
        for j in range(c0, min(c0 + _PAGE_CHUNK, n_pages)):
            ck = ckv_buf[slot, j].astype(BF16)
            scores.append((_dot_nt(ql, ck) + _dot(qp, kpe_buf[slot, j].astype(BF16))) * scale)
            values.append(ck)
        return scores, values

    starts = list(range(0, n_pages, _PAGE_CHUNK))
    pending = ([st], [kn[:, 0:KV_LORA]])
    parts = []
    for c0 in starts:
        nxt = chunk_scores(c0)
        parts.append(partial_softmax(*pending))
        pending = nxt
    parts.append(partial_softmax(*pending))
    m = parts[0][0]
    for mc, _, _ in parts[1:]:
        m = jnp.maximum(m, mc)
    l = jnp.zeros((rows, 1), F32)
    acc = jnp.zeros((rows, KV_LORA), F32)
    for mc, lsum, a in parts:
        w = jnp.exp(mc - m)
        l = l + w * _row_sum(lsum)
        acc = acc + w * a
    res = _dot((acc / l).astype(BF16), wuv_ref[...])
    headl = _lane_heads((8, BW), 1, V_A)
    out = jnp.zeros((8, BW), F32)
    for h in range(H_A):
        out = jnp.where(headl == h, res[h * 8:(h + 1) * 8, :], out)
    o_ref[0] = out


def _diff_decode(q_ref, kn_ref, vn_ref, bias_ref, dl_ref, sg_ref, o_ref, k_buf, v_buf, s_scr, knew, vnew, slot,
                 *, n_pages, page, ts, lam_init):
    scale = DH_D ** -0.5
    lam = _diff_lambda(dl_ref[...], lam_init)
    q8 = q_ref[0]
    lane = lax.broadcasted_iota(jnp.int32, (8, BW), 1)
    q64 = jnp.concatenate(
        [jnp.where(lane // DH_D == 2 * h + c, q8, 0.0) for c in range(2) for h in range(H_D)], axis=0).astype(BF16)
    rows = 64

    m_acc = jnp.full((rows, page), NEG_INF, F32)
    for j in range(n_pages):
        s = _dot(q64, k_buf[slot, j].astype(BF16)) * scale + bias_ref[:, j * page:(j + 1) * page]
        s_scr[:, j * page:(j + 1) * page] = s
        m_acc = jnp.maximum(m_acc, s)
    knew[0:ts, :] = kn_ref[0]
    vnew[0:ts, :] = vn_ref[0]
    tok = lax.broadcasted_iota(jnp.int32, (rows, page), 0) % 8
    lane_p = lax.broadcasted_iota(jnp.int32, (rows, page), 1)
    st = _dot_nt(q64, knew[...].astype(BF16)) * scale + bias_ref[:, n_pages * page:(n_pages + 1) * page]
    st = jnp.where(lane_p <= tok, st, NEG_INF)
    m = jnp.maximum(_row_max(m_acc), _row_max(st))

    l_acc = jnp.exp(st - m)
    acc = _dot(l_acc.astype(BF16), vnew[...].astype(BF16))
    for j in range(n_pages):
        p = jnp.exp(s_scr[:, j * page:(j + 1) * page] - m)
        l_acc = l_acc + p
        acc = acc + _dot_nt(p.astype(BF16), v_buf[slot, j].astype(BF16))
    o64 = acc / _row_sum(l_acc)
    acc = o64[0:32, :] - lam * o64[32:64, :]
    headl = _lane_heads((8, BW), 1, DV_D)
    out = jnp.zeros((8, BW), F32)
    for h in range(H_D):
        out = jnp.where(headl == h, acc[h * 8:(h + 1) * 8, :], out)
    ms = jnp.zeros_like(out)
    for h in range(H_D):
        mh = headl == h
        s = jnp.sum(jnp.where(mh, out * out, 0.0), axis=-1, keepdims=True) * (1.0 / DV_D)
        ms = jnp.where(mh, s, ms)
    o_ref[0] = out * lax.rsqrt(ms + EPS) * sg_ref[...] * (1.0 - lam_init)


def _paged_kernel(decode, n_in, pt_ref, *refs, layer, n_pages, **kw):
    ins, (hbm0, hbm1, o_ref, buf0, buf1, sem0, sem1), extra = refs[:n_in], refs[n_in:n_in + 7], refs[n_in + 7:]
    b = pl.program_id(0)

    @pl.when(b == 0)
    def _():
        for pad_page in extra[-kw["n_new"]:]:
            pad_page[...] = jnp.zeros_like(pad_page)

    def copies(bb, sl):
        return (_page_copies(pt_ref, hbm0, buf0, sem0, layer, bb, sl),
                _page_copies(pt_ref, hbm1, buf1, sem1, layer, bb, sl))

    slot = _prefetch_pages(copies, n_pages)
    for at in copies(b, slot):
        _wait_pages(at, n_pages)
    kw = {k: v for k, v in kw.items() if k != "n_new"}
    decode(*ins, o_ref, buf0, buf1, *extra, slot, n_pages=n_pages, **kw)


def _per_request(*shape):
    return pl.BlockSpec((1,) + shape, lambda b, pt: (b,) + (0,) * len(shape))


def _mla_sample(page_table, qc, qp, ckvn, kpen, wuv_all, cache_ckv, cache_kpeT, i):
    nb, n_pages = page_table.shape
    page = cache_ckv.shape[2]
    ts = ckvn.shape[1]
    hbm = pl.BlockSpec(memory_space=pl.ANY)
    grid_spec = pltpu.PrefetchScalarGridSpec(
        num_scalar_prefetch=1,
        grid=(nb,),
        in_specs=[_per_request(32, 2 * KV_LORA), _per_request(32, ROPE_A), _per_request(ts, KV_LORA),
                  _per_request(ts, KV_LORA), pl.BlockSpec((None, KV_LORA, BW), lambda b, pt: (i, 0, 0)), hbm, hbm],
        out_specs=_per_request(8, BW),
        scratch_shapes=[
            pltpu.VMEM((2, n_pages, page, KV_LORA), F32),
            pltpu.VMEM((2, n_pages, ROPE_A, page), F32),
            pltpu.SemaphoreType.DMA((2,)),
            pltpu.SemaphoreType.DMA((2,)),
            pltpu.VMEM((page, 2 * KV_LORA), F32),
        ],
    )
    return pl.pallas_call(
        functools.partial(_paged_kernel, _mla_decode, 5, layer=i, n_pages=n_pages, page=page, ts=ts, n_new=1),
        grid_spec=grid_spec,
        out_shape=jax.ShapeDtypeStruct((nb, 8, BW), F32),
        compiler_params=_cp("arbitrary"),
        name="mla_sample",
    )(page_table, qc, qp, ckvn, kpen, wuv_all, cache_ckv, cache_kpeT)


def _diff_sample(page_table, q8, kn, vn, bias, dl, sg4, cache_kT, cache_vT, i, lam_init):
    nb, n_pages = page_table.shape
    page = cache_kT.shape[3]
    ts = kn.shape[1]
    hbm = pl.BlockSpec(memory_space=pl.ANY)
    grid_spec = pltpu.PrefetchScalarGridSpec(
        num_scalar_prefetch=1,
        grid=(nb,),
        in_specs=[_per_request(8, BW), _per_request(ts, BW), _per_request(ts, BW),
                  pl.BlockSpec((64, (n_pages + 1) * page), lambda b, pt: (0, 0)),
                  pl.BlockSpec((None, 4, DH_D), lambda b, pt: (i, 0, 0)),
                  pl.BlockSpec((1, BW), lambda b, pt: (0, 0)), hbm, hbm],
        out_specs=_per_request(8, BW),
        scratch_shapes=[
            pltpu.VMEM((2, n_pages, BW, page), F32),
            pltpu.VMEM((2, n_pages, BW, page), F32),
            pltpu.SemaphoreType.DMA((2,)),
            pltpu.SemaphoreType.DMA((2,)),
            pltpu.VMEM((64, n_pages * page), F32),
            pltpu.VMEM((page, BW), F32),
            pltpu.VMEM((page, BW), F32),
        ],
    )
    return pl.pallas_call(
        functools.partial(_paged_kernel, _diff_decode, 6, layer=i, n_pages=n_pages, page=page, ts=ts,
                          lam_init=lam_init, n_new=2),
        grid_spec=grid_spec,
        out_shape=jax.ShapeDtypeStruct((nb, 8, BW), F32),
        compiler_params=_cp("arbitrary"),
        name="diff_sample",
    )(page_table, q8, kn, vn, bias, dl, sg4, cache_kT, cache_vT)


def _merge_kernel(x_ref, *refs, n_ptiles):
    br_p, br_s = refs[0:N_BRANCH], refs[N_BRANCH:2 * N_BRANCH]
    g_ref, wg_ref, wb_ref, wo_ref, o_ref = refs[2 * N_BRANCH:]
    is_prompt = pl.program_id(0) < n_ptiles
    x = x_ref[...]
    d = x.shape[1]
    un = _rms(x, g_ref[0:1, :]).astype(BF16)
    mix = jnp.zeros_like(x)
    for n in range(N_BRANCH):
        gate = _dot(un, wg_ref[:, n * d:(n + 1) * d])
        br = jnp.where(is_prompt, br_p[n][...], br_s[n][...])
        mix = mix + _sigmoid(gate) * _dot(br.astype(BF16), wb_ref[n])
    o_ref[...] = x + _rms(_dot(mix.astype(BF16), wo_ref[...]), g_ref[1:2, :])


def _merge(h, br_p, br_s, g2, wgates, wbr, wout, i, tm):
    n, d = h.shape
    n_ptiles = br_p[0].shape[0] // tm
    row_p = pl.BlockSpec((tm, BW), lambda t: (jnp.minimum(t, n_ptiles - 1), 0))
    row_s = pl.BlockSpec((tm, BW), lambda t: (jnp.maximum(t - n_ptiles, 0), 0))
    return pl.pallas_call(
        functools.partial(_merge_kernel, n_ptiles=n_ptiles),
        grid=(n // tm,),
        in_specs=[
            pl.BlockSpec((tm, d), lambda t: (t, 0))] + [row_p] * N_BRANCH + [row_s] * N_BRANCH + [
            pl.BlockSpec((2, d), lambda t: (0, 0)),
            pl.BlockSpec((None, d, N_BRANCH * d), lambda t: (i, 0, 0)),
            pl.BlockSpec((None, N_BRANCH, BW, d), lambda t: (i, 0, 0, 0)),
            pl.BlockSpec((None, d, d), lambda t: (i, 0, 0)),
        ],
        out_specs=pl.BlockSpec((tm, d), lambda t: (t, 0)),
        out_shape=jax.ShapeDtypeStruct((n, d), F32),
        compiler_params=_cp("parallel"),
        name="merge",
    )(h, *br_p, *br_s, g2, wgates, wbr, wout)


def _ple_kernel(x_ref, p_ref, g_ref, wg_ref, wi_ref, o_ref):
    x = x_ref[...]
    gate = _sigmoid(_dot(_rms(x, g_ref[0:1, :]).astype(BF16), wg_ref[...]))
    e = gate * _dot(p_ref[...].astype(BF16), wi_ref[...])
    o_ref[...] = x + _rms(e, g_ref[1:2, :])


def _ple(h, p, g2, wpg, wpi, i, tm):
    n, d = h.shape
    pd = p.shape[-1]
    return pl.pallas_call(
        _ple_kernel,
        grid=(n // tm,),
        in_specs=[
            pl.BlockSpec((tm, d), lambda t: (t, 0)),
            pl.BlockSpec((None, tm, pd), lambda t: (i, t, 0)),
            pl.BlockSpec((2, d), lambda t: (0, 0)),
            pl.BlockSpec((None, d, d), lambda t: (i, 0, 0)),
            pl.BlockSpec((None, pd, d), lambda t: (i, 0, 0)),
        ],
        out_specs=pl.BlockSpec((tm, d), lambda t: (t, 0)),
        out_shape=jax.ShapeDtypeStruct((n, d), F32),
        compiler_params=_cp("parallel"),
        name="ple",
    )(h, p, g2, wpg, wpi)


def _rope_table(pos):
    def cs(d, reps):
        half = d // 2
        freqs = ROPE_THETA ** (-jnp.arange(half, dtype=F32) / half)
        ang = pos.astype(F32)[:, None] * freqs[None, :]
        c, s = jnp.cos(ang), jnp.sin(ang)
        return jnp.tile(jnp.concatenate([c, c], -1), (1, reps)), jnp.tile(jnp.concatenate([-s, s], -1), (1, reps))
    cc, sc = cs(DK_C, H_C)
    ca, sa = cs(ROPE_A, 4)
    return jnp.concatenate([cc, sc, ca, sa], axis=-1)


def _t5_bucket(n):
    max_exact = T5_BUCKETS // 2
    nf = jnp.maximum(n, 1).astype(F32)
    large = max_exact + (jnp.log(nf / max_exact) / math.log(T5_MAX_DIST / max_exact)
                         * (T5_BUCKETS - max_exact)).astype(jnp.int32)
    large = jnp.minimum(large, T5_BUCKETS - 1)
    return jnp.where(n < max_exact, n, large)


def kernel(x_prompt, x_sample, cache_mla_ckv, cache_mla_kpe, cache_diff_k, cache_diff_v, state_lru_h, state_lru_conv, state_ret, page_table, p_prompt, p_sample, t5_bias, norm_g, w_ffn_gu, w_ffn_down, w_in, w_uq, g_q, g_kv, w_uk, w_uv, conv_w, conv_b, lru_wa, lru_ba, lru_wx, lru_bx, lru_lambda, ret_gn_g, diff_lambda, diff_subln_g, w_branch, w_out, w_ple_in, w_ple_gate):
    bp, t, d = x_prompt.shape
    bs, ts, _ = x_sample.shape
    depth = w_in.shape[0]
    n_pages = page_table.shape[1]
    page = cache_mla_ckv.shape[2]
    past_len = n_pages * page
    np_, ns_ = bp * t, bs * ts
    n = np_ + ns_
    ff = w_ffn_down.shape[2]

    tm = min(512, ns_)
    assert t % tm == 0 and ns_ % tm == 0
    fc = ff // 2 if (ff // 2) % LANES == 0 else ff
    tq = min(256, t)
    rc = min(256, t)
    assert ts <= 8 and t % tq == 0 and t % RET_CHUNK == 0 and t % rc == 0

    wgu_b = w_ffn_gu.astype(BF16)
    wdn_b = w_ffn_down.astype(BF16)
    w1 = _take_cols(w_in, _w1_columns()).astype(BF16)
    wgates = w_in[:, :, sum((256, 128, 32, 256, 256, 256, 256, 256, 256, 256, 256, 256)):].astype(BF16)
    hq = NOPE_A + ROPE_A
    nope_cols = np.concatenate([h * hq + np.arange(NOPE_A) for h in range(H_A)])
    pe_cols = np.concatenate([h * hq + NOPE_A + np.arange(ROPE_A) for h in range(H_A)])
    pes_cols = np.concatenate([h * hq + NOPE_A + _swap_halves(ROPE_A, ROPE_A) for h in range(H_A)])
    wuq = _take_cols(w_uq, np.concatenate([nope_cols, pe_cols, pes_cols])).astype(BF16)
    eye = jnp.eye(H_A, dtype=F32)
    wuk_t = jnp.transpose(w_uk, (0, 2, 3, 1))
    wuk_bd = (eye[None, :, None, :, None] * wuk_t[:, :, :, None, :]).reshape(depth, H_A * NOPE_A, H_A * KV_LORA).astype(BF16)
    wuv_t = jnp.transpose(w_uv, (0, 2, 1, 3))
    wuv_pad = (wuv_t[:, :, :, None, :] * eye[None, :, None, :, None]).reshape(depth, H_A, KV_LORA, BW).astype(BF16)
    wuv_all = w_uv.reshape(depth, KV_LORA, H_A * V_A).astype(BF16)
    eye_l = jnp.eye(LRU_BLOCKS, dtype=F32)
    bdiag = lambda w: (eye_l[None, :, None, :, None] * w[:, :, :, None, :]).reshape(depth, LRU_W, LRU_W).astype(BF16)
    wa_bd, wx_bd = bdiag(lru_wa), bdiag(lru_wx)
    lru_vec = jnp.stack([conv_b, lru_ba, lru_bx, lru_lambda], axis=1)
    wbr_b = w_branch.astype(BF16)
    wout_b = w_out.astype(BF16)
    wpg_b = w_ple_gate.astype(BF16)
    wpi_b = w_ple_in.astype(BF16)
    subln4 = jnp.tile(diff_subln_g, (1, H_D))
    gn_b = jnp.broadcast_to(ret_gn_g[:, :, None], (depth, BW, bs))

    pos_p = jnp.arange(t, dtype=jnp.int32)
    pos_s = past_len + jnp.arange(ts, dtype=jnp.int32)
    tab = jnp.concatenate([_rope_table(pos_p), _rope_table(jnp.tile(pos_s, tm // ts))], axis=0)
    tiles_per_seq = t // tm
    n_ptiles = np_ // tm
    tab_index = lambda i: jnp.where(i < n_ptiles, i % tiles_per_seq, tiles_per_seq)

    nq = t // tq
    r_ = jnp.arange(tq, dtype=jnp.int32)
    dist = jnp.maximum(jnp.arange(nq, dtype=jnp.int32)[:, None, None] * tq + r_[None, :, None] - r_[None, None, :], 0)
    bias_p = _t5_expand(t5_bias, _t5_bucket(dist).reshape(nq * tq, tq), min(64, tq)).reshape(H_D, nq, tq, tq)
    k_pos_s = jnp.arange(past_len + page, dtype=jnp.int32)
    pos_s8 = past_len + jnp.arange(8, dtype=jnp.int32)
    dist_s = jnp.maximum(pos_s8[:, None] - k_pos_s[None, :], 0)
    bias_s = _t5_expand(t5_bias, _t5_bucket(dist_s), 8).reshape(H_D * 8, past_len + page)
    bias_s = jnp.concatenate([bias_s, bias_s], axis=0)

    cache_kpeT = jnp.transpose(cache_mla_kpe, (0, 1, 3, 2))
    cache_dkT = jnp.transpose(cache_diff_k, (0, 1, 3, 4, 2)).reshape(depth, -1, BW, page)
    cache_dvT = jnp.transpose(cache_diff_v, (0, 1, 3, 4, 2)).reshape(depth, -1, BW, page)
    state_retT = jnp.transpose(state_ret, (0, 2, 3, 4, 1))
    conv_sT = jnp.transpose(state_lru_conv, (0, 2, 1, 3))

    h = jnp.concatenate([x_prompt.reshape(np_, d), x_sample.reshape(ns_, d)], axis=0)
    p_all = jnp.concatenate([p_prompt.reshape(depth, np_, -1), p_sample.reshape(depth, ns_, -1)], axis=1)

    def to_tb(a):
        return jnp.transpose(a.reshape(bs, ts, -1), (1, 0, 2))

    def to_twb(a):
        return jnp.transpose(a.reshape(bs, ts, -1), (1, 2, 0)).astype(F32)

    head_mask = (jnp.arange(H_A * ROPE_A)[None, :] // ROPE_A == jnp.arange(H_A)[:, None]).astype(BF16)

    outs = [[] for _ in range(14)]
    for i in range(depth):
        lam_init = 0.8 - 0.6 * math.exp(-0.3 * i)
        ng = norm_g[i]
        h = _ffn(h, ng[0:2], wgu_b, wdn_b, i, 0, tm, fc)
        (ckvn, kpe4, kcat, qlat, qpe, lgate, lx, rq, rk, rv, rg, dq, dk, dv, dkb, dvb) = _inproj(
            h, ng[2:3], w1, tab, g_q[i:i + 1], g_kv[i:i + 1], wuq, wuk_bd, i, tm, tab_index)
        kpe = kpe4[:, :ROPE_A]

        oa_p = _mla_prompt(qlat, qpe, kcat, wuv_pad, i, bp, t, tq)
        od_p = _diff_prompt(dq, dkb, dvb, bias_p, diff_lambda, diff_subln_g[i:i + 1], i, bp, t, tq, lam_init)
        oc_p, s_full = _ret_prompt(rq, rk, rv, rg, ret_gn_g[i:i + 1], i, bp, t)
        ob_p, hl_p = _lru_prompt(lx, lgate, conv_w, lru_vec, wa_bd, wx_bd, i, bp, t, rc)

        sl = slice(np_, n)
        pad_t = ((0, 0), (0, 0), (0, 8 - ts), (0, 0))
        ql_s = jnp.transpose(qlat[sl].reshape(bs, ts, H_A, KV_LORA), (0, 2, 1, 3))
        qp_rep = qpe[sl].reshape(bs, 1, ts, H_A * ROPE_A) * head_mask[None, :, None, :]
        qc_s = jnp.pad(jnp.concatenate([ql_s, qp_rep], axis=-1), pad_t).reshape(bs, H_A * 8, 2 * KV_LORA)
        qp_s = jnp.pad(jnp.transpose(qpe[sl].reshape(bs, ts, H_A, ROPE_A), (0, 2, 1, 3)),
                       pad_t).reshape(bs, H_A * 8, ROPE_A)
        oa_s = _mla_sample(page_table, qc_s, qp_s, ckvn[sl].reshape(bs, ts, KV_LORA),
                           kpe4[sl].reshape(bs, ts, KV_LORA), wuv_all, cache_mla_ckv, cache_kpeT, i)
        dq8 = jnp.pad(dq[sl].reshape(bs, ts, BW), ((0, 0), (0, 8 - ts), (0, 0))).astype(F32)
        od_s = _diff_sample(page_table, dq8, dk[sl].reshape(bs, ts, BW), dv[sl].reshape(bs, ts, BW), bias_s,
                            diff_lambda, subln4[i:i + 1], cache_dkT, cache_dvT, i, lam_init)
        oc_sT, s_new = _ret_sample(to_twb(rq[sl]), to_twb(rk[sl]), to_twb(rv[sl]), to_twb(rg[sl]),
                                   gn_b[i], state_retT, i)
        ob_s, hl_s = _lru_sample(to_tb(lx[sl]), to_tb(lgate[sl]), conv_sT, state_lru_h, conv_w, lru_vec,
                                 wa_bd, wx_bd, i)

        br_s = (oa_s[:, :ts].reshape(ns_, BW), jnp.transpose(ob_s, (1, 0, 2)).reshape(ns_, BW),
                jnp.transpose(oc_sT, (2, 0, 1)).reshape(ns_, BW), od_s[:, :ts].reshape(ns_, BW))
        h = _merge(h, (oa_p, ob_p, oc_p, od_p), br_s, ng[2:4], wgates, wbr_b, wout_b, i, tm)
        h = _ffn(h, ng[4:6], wgu_b, wdn_b, i, 1, tm, fc)
        h = _ple(h, p_all, ng[6:8], wpg_b, wpi_b, i, tm)

        lx_p = lx[:np_].reshape(bp, t, LRU_W)
        s5 = s_full.reshape(bp, H_C, DK_C, H_C, DV_C)
        ret_p = jnp.stack([s5[:, hh, :, hh, :] for hh in range(H_C)], axis=1)
        vals = [ckvn[:np_].reshape(bp, t, KV_LORA), kpe[:np_].reshape(bp, t, ROPE_A),
                dk[:np_].reshape(bp, t, H_D, 2 * DH_D), dv[:np_].reshape(bp, t, H_D, DV_D),
                hl_p.reshape(bp, LRU_W), lx_p[:, t - (CONV_W - 1):], ret_p,
                ckvn[sl].reshape(bs, ts, KV_LORA), kpe[sl].reshape(bs, ts, ROPE_A),
                dk[sl].reshape(bs, ts, H_D, 2 * DH_D), dv[sl].reshape(bs, ts, H_D, DV_D),
                hl_s, lx[sl].reshape(bs, ts, LRU_W)[:, ts - (CONV_W - 1):],
                jnp.transpose(s_new, (3, 0, 1, 2))]
        for lst, v in zip(outs, vals):
            lst.append(v)

    stacked = [jnp.stack(a) for a in outs]
    return (h[:np_].reshape(bp, t, d), h[np_:].reshape(bs, ts, d), *stacked)
```
